```python
import math
import jax, jax.numpy as jnp
from jax import lax
import numpy as np

D_MODEL = 2048
BATCH = 4
SEQ = 2048
DEPTH = 1

HEAD_DIM = 128
MIX_WIDTH = D_MODEL
A_HEADS = MIX_WIDTH // (2 * HEAD_DIM)
B_HEADS = MIX_WIDTH // (2 * HEAD_DIM)
A_WIDTH = A_HEADS * HEAD_DIM
B_WIDTH = B_HEADS * HEAD_DIM
DIFF_DIM = HEAD_DIM // 2
IN_WIDTH = 3 * A_WIDTH + 3 * B_WIDTH
DILATED_CONFIGS = ((128, 1), (512, 4), (2048, 16))
ROPE_THETA = 500000.0
ROPE_FRACTION = 4
D_FF = ((-(-8 * D_MODEL // 3) + 255) // 256) * 256
Q_BLOCK = 128
RMS_EPS = 1e-6
SUBLN_EPS = 1e-5
NEG_INF = -1e30

kernel_name = 'hybrid_dilated_diff_attn_encoder_block'


def rmsnorm(x, g, eps=RMS_EPS):
    xf = x.astype(jnp.float32)
    y = xf * lax.rsqrt(jnp.mean(xf * xf, axis=-1, keepdims=True) + eps)
    return (y * g.astype(jnp.float32)).astype(x.dtype)


def rope_tables(seq, rot_dim):
    inv_freq = ROPE_THETA ** (-jnp.arange(0, rot_dim, 2, dtype=jnp.float32) / rot_dim)
    ang = jnp.arange(seq, dtype=jnp.float32)[:, None] * inv_freq[None, :]
    return jnp.cos(ang), jnp.sin(ang)


def rope(x, cos, sin):
    r2 = cos.shape[-1]
    shape = (x.shape[1],) + (1,) * (x.ndim - 3) + (r2,)
    c, s = cos.reshape(shape), sin.reshape(shape)
    x1, x2 = x[..., :r2], x[..., r2:2 * r2]
    return jnp.concatenate([x1 * c - x2 * s, x2 * c + x1 * s, x[..., 2 * r2:]], axis=-1).astype(x.dtype)


def dilated_branch(q, k, v, dil, half):
    B, S, H, Dh = q.shape
    L = S // dil
    nb = -(-L // half)
    Lp = nb * half

    def by_residue(a):
        return a.reshape(B, L, dil, H, Dh).transpose(0, 2, 3, 1, 4)

    qs = jnp.pad(by_residue(q), ((0, 0), (0, 0), (0, 0), (0, Lp - L), (0, 0))).reshape(B, dil, H, nb, half, Dh)
    kpad = ((0, 0), (0, 0), (0, 0), (half, Lp - L + half), (0, 0))
    kp = jnp.pad(by_residue(k), kpad).reshape(B, dil, H, nb + 2, half, Dh)
    vp = jnp.pad(by_residue(v), kpad).reshape(B, dil, H, nb + 2, half, Dh)

    def band(a):
        return jnp.concatenate([a[:, :, :, :-2], a[:, :, :, 1:-1], a[:, :, :, 2:]], axis=4)

    kb, vb = band(kp), band(vp)
    s = jnp.einsum('brhnqe,brhnke->brhnqk', qs, kb, preferred_element_type=jnp.float32) * (Dh ** -0.5)
    blk = jnp.arange(nb)[:, None, None]
    qa = jnp.arange(half)[None, :, None]
    kj = jnp.arange(3 * half)[None, None, :]
    kpos = blk * half - half + kj
    dist = kj - half - qa
    mask = (jnp.abs(dist) <= half) & (kpos >= 0) & (kpos < L)
    s = jnp.where(mask, s, NEG_INF)
    m = jnp.max(s, axis=-1, keepdims=True)
    p = jnp.exp(s - m)
    den = jnp.sum(p, axis=-1)
    o = jnp.einsum('brhnqk,brhnke->brhnqe', p, vb.astype(jnp.float32)) / den[..., None]
    lse = m[..., 0] + jnp.log(den)
    o = o.reshape(B, dil, H, Lp, Dh)[:, :, :, :L].transpose(0, 3, 1, 2, 4).reshape(B, S, H, Dh)
    lse = lse.reshape(B, dil, H, Lp)[:, :, :, :L].transpose(0, 3, 1, 2).reshape(B, S, H)
    return o, lse


def dilated_attention(q, k, v):
    outs, lses = [], []
    for window, dil in DILATED_CONFIGS:
        o, lse = dilated_branch(q, k, v, dil, window // (2 * dil))
        outs.append(o)
        lses.append(lse)
    w = jax.nn.softmax(jnp.stack(lses, axis=0), axis=0)
    return jnp.einsum('gbsh,gbshe->bshe', w, jnp.stack(outs, axis=0))


def diff_attention(q, k, v, lam):
    B, S, H, _, d = q.shape
    nqb = S // Q_BLOCK
    qb = q.reshape(B, nqb, Q_BLOCK, H, 2, d).transpose(1, 0, 3, 4, 2, 5)
    kt = k.transpose(0, 2, 3, 1, 4)
    vt = v.transpose(0, 2, 1, 3)

    def one_block(qblk):
        s = jnp.einsum('bhcqe,bhcke->bhcqk', qblk, kt, preferred_element_type=jnp.float32) * (d ** -0.5)
        p = jax.nn.softmax(s, axis=-1)
        a = p[:, :, 0] - lam * p[:, :, 1]
        return jnp.einsum('bhqk,bhke->bhqe', a, vt.astype(jnp.float32))

    o = lax.map(one_block, qb)
    return o.transpose(1, 0, 3, 2, 4).reshape(B, S, H, 2 * d)


def setup_inputs(seed: int = 0) -> dict:
    key = jax.random.key(seed)
    ks = jax.random.split(key, 11)
    f32 = jnp.float32

    def dense(k, shape):
        return jax.random.normal(k, shape, f32) * (shape[-2] ** -0.5)

    def gain(k, shape):
        return 1.0 + 0.02 * jax.random.normal(k, shape, f32)

    return {
        'x': jax.random.normal(ks[0], (BATCH, SEQ, D_MODEL), f32),
        'norm_attn': gain(ks[1], (DEPTH, D_MODEL)),
        'w_in': dense(ks[2], (DEPTH, D_MODEL, IN_WIDTH)),
        'lambda_qk': 0.1 * jax.random.normal(ks[3], (DEPTH, 4, DIFF_DIM), f32),
        'subln': gain(ks[4], (DEPTH, 2 * DIFF_DIM)),
        'w_out': dense(ks[5], (DEPTH, MIX_WIDTH, D_MODEL)),
        'norm_ffn': gain(ks[6], (DEPTH, D_MODEL)),
        'w_gate': dense(ks[7], (DEPTH, D_MODEL, D_FF)),
        'w_up': dense(ks[8], (DEPTH, D_MODEL, D_FF)),
        'w_down': dense(ks[9], (DEPTH, D_FF, D_MODEL)),
        'norm_final': gain(ks[10], (D_MODEL,)),
    }


def reference(x, norm_attn, w_in, lambda_qk, subln, w_out, norm_ffn, w_gate, w_up, w_down, norm_final):
    B, S, _ = x.shape
    cos_a, sin_a = rope_tables(S, HEAD_DIM // ROPE_FRACTION)
    cos_b, sin_b = rope_tables(S, DIFF_DIM // ROPE_FRACTION)
    splits = [A_WIDTH, 2 * A_WIDTH, 3 * A_WIDTH, 3 * A_WIDTH + B_WIDTH, 3 * A_WIDTH + 2 * B_WIDTH]
    for l in range(DEPTH):
        h = rmsnorm(x, norm_attn[l])
        proj = h @ w_in[l]
        qa, ka, va, qb, kb, vb = jnp.split(proj, splits, axis=-1)
        qa = rope(qa.reshape(B, S, A_HEADS, HEAD_DIM), cos_a, sin_a)
        ka = rope(ka.reshape(B, S, A_HEADS, HEAD_DIM), cos_a, sin_a)
        va = va.reshape(B, S, A_HEADS, HEAD_DIM)
        ya = dilated_attention(qa, ka, va).astype(x.dtype).reshape(B, S, A_WIDTH)
        lam_init = 0.8 - 0.6 * math.exp(-0.3 * l)
        lq = lambda_qk[l].astype(jnp.float32)
        lam = jnp.exp(jnp.sum(lq[0] * lq[1])) - jnp.exp(jnp.sum(lq[2] * lq[3])) + lam_init
        qb = rope(qb.reshape(B, S, B_HEADS, 2, DIFF_DIM), cos_b, sin_b)
        kb = rope(kb.reshape(B, S, B_HEADS, 2, DIFF_DIM), cos_b, sin_b)
        vb = vb.reshape(B, S, B_HEADS, 2 * DIFF_DIM)
        yb = diff_attention(qb, kb, vb, lam)
        yb = (rmsnorm(yb, subln[l], SUBLN_EPS) * (1.0 - lam_init)).astype(x.dtype).reshape(B, S, B_WIDTH)
        x = x + jnp.concatenate([ya, yb], axis=-1) @ w_out[l]
        h = rmsnorm(x, norm_ffn[l])
        x = x + (jax.nn.silu(h @ w_gate[l]) * (h @ w_up[l])) @ w_down[l]
    return rmsnorm(x, norm_final)
```

```python
import functools
import math

import jax
import jax.numpy as jnp
from jax import lax
from jax.experimental import pallas as pl
from jax.experimental.pallas import tpu as pltpu

D_MODEL = 2048
HEAD_DIM = 128
N_HEADS = 8
GROUP_WIDTH = N_HEADS * HEAD_DIM
DIFF_DIM = HEAD_DIM // 2
D_FF = 5632
ROPE_THETA = 500000.0
RMS_EPS = 1e-6
SUBLN_EPS = 1e-5
NEG_BIAS = -1e30
HALF_WINDOW = 64
DILATIONS = (1, 4, 16)
LAM_INIT = 0.8 - 0.6 * math.exp(-0.3 * 0)
LOG2E = math.log2(math.e)

VMEM_LIMIT_BYTES = 56 * 1024 * 1024
LANES = 128

BF16 = jnp.bfloat16
F32 = jnp.float32


def _params(semantics):
    return pltpu.CompilerParams(dimension_semantics=semantics, vmem_limit_bytes=VMEM_LIMIT_BYTES)


def _rms_scale(x, eps):
    return x * lax.rsqrt(jnp.mean(x * x, axis=-1, keepdims=True) + eps)


def _rope_tables(seq):
    pos = jnp.arange(seq, dtype=F32)[:, None]
    lane = jnp.arange(LANES)

    def tables(period, rot_dim, scale):
        half = rot_dim // 2
        inv_freq = ROPE_THETA ** (-jnp.arange(0, rot_dim, 2, dtype=F32) / rot_dim)
        ang = pos * inv_freq[None, :]
        cos, sin = jnp.cos(ang), jnp.sin(ang)
        j = lane % period
        idx = j % half
        in_lo = j < half
        in_hi = (j >= half) & (j < rot_dim)
        cos_l = jnp.take(cos, idx, axis=1)
        sin_l = jnp.take(sin, idx, axis=1)
        c = jnp.where((in_lo | in_hi)[None, :], cos_l, 1.0)
        s1 = jnp.where(in_lo[None, :], -sin_l, 0.0)
        s2 = jnp.where(in_hi[None, :], sin_l, 0.0)
        return jnp.stack([c, s1, s2]) * scale

    scale_a = HEAD_DIM ** -0.5 * LOG2E
    scale_b = DIFF_DIM ** -0.5 * LOG2E
    t = jnp.stack([
        tables(HEAD_DIM, HEAD_DIM // 4, scale_a),
        tables(HEAD_DIM, HEAD_DIM // 4, 1.0),
        tables(DIFF_DIM, DIFF_DIM // 4, scale_b),
        tables(DIFF_DIM, DIFF_DIM // 4, 1.0),
    ])
    return t[:, 0], t[:, 1], t[:, 2]


def _in_proj_kernel(x_ref, g_ref, w_ref, c_ref, s1_ref, s2_ref, oa_ref, ob_ref, h_ref):
    n = pl.program_id(1)

    @pl.when(n == 0)
    def _():
        h_ref[...] = (_rms_scale(x_ref[...], RMS_EPS) * g_ref[...]).astype(BF16)

    acc = jnp.dot(h_ref[...], w_ref[...], preferred_element_type=F32)

    def rope(shift):
        c, s1, s2 = c_ref[0], s1_ref[0], s2_ref[0]
        heads = []
        for h in range(N_HEADS):
            a = acc[:, h * HEAD_DIM:(h + 1) * HEAD_DIM]
            heads.append(a * c
                         + pltpu.roll(a, HEAD_DIM - shift, 1) * s1
                         + pltpu.roll(a, shift, 1) * s2)
        return jnp.concatenate(heads, axis=1)

    @pl.when(n < 2)
    def _():
        oa_ref[...] = rope(HEAD_DIM // 8)

    @pl.when(n == 2)
    def _():
        oa_ref[...] = acc

    @pl.when((n == 3) | (n == 4))
    def _():
        ob_ref[...] = rope(DIFF_DIM // 8).astype(BF16)

    @pl.when(n == 5)
    def _():
        ob_ref[...] = acc.astype(BF16)


def _in_proj(x2d, gain, w_bf16, tabs, seq, tm=1024):
    t = x2d.shape[0]
    c_t, s1_t, s2_t = tabs
    seq_tiles = seq // tm
    tab_idx = lambda m, n: (jnp.where(n < 2, n, jnp.clip(n - 1, 2, 3)), m % seq_tiles, 0)
    tab_spec = pl.BlockSpec((1, tm, LANES), tab_idx)
    return pl.pallas_call(
        _in_proj_kernel,
        grid=(t // tm, 6),
        in_specs=[
            pl.BlockSpec((tm, D_MODEL), lambda m, n: (m, 0)),
            pl.BlockSpec((1, D_MODEL), lambda m, n: (0, 0)),
            pl.BlockSpec((D_MODEL, GROUP_WIDTH), lambda m, n: (0, n)),
            tab_spec, tab_spec, tab_spec,
        ],
        out_specs=[
            pl.BlockSpec((tm, GROUP_WIDTH), lambda m, n: (m, jnp.minimum(n, 2))),
            pl.BlockSpec((tm, GROUP_WIDTH), lambda m, n: (m, jnp.maximum(n - 3, 0))),
        ],
        out_shape=[
            jax.ShapeDtypeStruct((t, 3 * GROUP_WIDTH), F32),
            jax.ShapeDtypeStruct((t, 3 * GROUP_WIDTH), BF16),
        ],
        scratch_shapes=[pltpu.VMEM((tm, D_MODEL), BF16)],
        compiler_params=_params(("arbitrary", "arbitrary")),
        name="in_proj",
    )(x2d, gain, w_bf16, c_t, s1_t, s2_t)


def _window_bias(n_q, n_k, offset):
    row = lax.broadcasted_iota(jnp.int32, (n_q, n_k), 0)
    col = lax.broadcasted_iota(jnp.int32, (n_q, n_k), 1)
    dist = col + offset - row
    return jnp.where(jnp.abs(dist) <= HALF_WINDOW, 0.0, NEG_BIAS).astype(F32)


def _attn_a_kernel(q_ref, k_ref, v_ref, o_ref, qs_ref, ks_ref, vs_ref, ob_ref, lb_ref, *, seq):
    tq = 128
    for g, dil in enumerate(DILATIONS):
        length = seq // dil
        kw = min(256, length)
        for r in range(dil):
            rows = pl.ds(r, length, stride=dil) if dil > 1 else pl.ds(0, length)
            dst = pl.ds(r * length, length)
            qs_ref[dst, :] = q_ref[rows, :].astype(BF16)
            ks_ref[dst, :] = k_ref[rows, :].astype(BF16)
            vs_ref[dst, :] = v_ref[rows, :].astype(BF16)
        for r in range(dil):
            for tile in range(length // tq):
                q0 = tile * tq
                ws = min(max(q0 - HALF_WINDOW, 0), length - kw)
                base = r * length
                qt = qs_ref[pl.ds(base + q0, tq), :]
                kt = ks_ref[pl.ds(base + ws, kw), :]
                vt = vs_ref[pl.ds(base + ws, kw), :]
                s = lax.dot_general(qt, kt, (((1,), (1,)), ((), ())), preferred_element_type=F32)
                s = s + _window_bias(tq, kw, ws - q0)
                m = jnp.max(s, axis=-1, keepdims=True)
                p = jnp.exp2(s - m)
                den = jnp.sum(p, axis=-1, keepdims=True)
                o = jnp.dot(p.astype(BF16), vt, preferred_element_type=F32) / den
                lse = m + jnp.log2(den)
                out_rows = pl.ds(q0 * dil + r, tq, stride=dil) if dil > 1 else pl.ds(q0, tq)
                ob_ref[g, out_rows, :] = o
                lb_ref[g, out_rows, :] = jnp.broadcast_to(lse, (tq, LANES))
    l0, l1, l2 = lb_ref[0], lb_ref[1], lb_ref[2]
    mx = jnp.maximum(jnp.maximum(l0, l1), l2)
    e0, e1, e2 = jnp.exp2(l0 - mx), jnp.exp2(l1 - mx), jnp.exp2(l2 - mx)
    merged = (e0 * ob_ref[0] + e1 * ob_ref[1] + e2 * ob_ref[2]) / (e0 + e1 + e2)
    o_ref[...] = merged.astype(BF16)


def _attn_a(proj_a, batch, seq):
    t = batch * seq
    spec = lambda col0: pl.BlockSpec((seq, HEAD_DIM), lambda b, h: (b, col0 + h))
    return pl.pallas_call(
        functools.partial(_attn_a_kernel, seq=seq),
        grid=(batch, N_HEADS),
        in_specs=[spec(0), spec(N_HEADS), spec(2 * N_HEADS)],
        out_specs=pl.BlockSpec((seq, HEAD_DIM), lambda b, h: (b, h)),
        out_shape=jax.ShapeDtypeStruct((t, GROUP_WIDTH), BF16),
        scratch_shapes=[
            pltpu.VMEM((seq, HEAD_DIM), BF16),
            pltpu.VMEM((seq, HEAD_DIM), BF16),
            pltpu.VMEM((seq, HEAD_DIM), BF16),
            pltpu.VMEM((len(DILATIONS), seq, HEAD_DIM), F32),
            pltpu.VMEM((len(DILATIONS), seq, LANES), F32),
        ],
        compiler_params=_params(("arbitrary", "arbitrary")),
        name="attn_a",
    )(proj_a, proj_a, proj_a)


def _attn_b_kernel(q_ref, k_ref, v_ref, lq_ref, g_ref, o_ref, vext_ref):
    @pl.when(pl.program_id(2) == 0)
    def _():
        vext_ref[:, :HEAD_DIM] = v_ref[...]
        vext_ref[:, HEAD_DIM:] = jnp.ones(v_ref.shape, BF16)

    lq = lq_ref[...]
    lam = (jnp.exp(jnp.sum(lq[0:1] * lq[1:2], axis=-1, keepdims=True))
           - jnp.exp(jnp.sum(lq[2:3] * lq[3:4], axis=-1, keepdims=True)) + LAM_INIT)

    q = q_ref[...]
    k = k_ref[...]
    lane = lax.broadcasted_iota(jnp.int32, q.shape, 1)
    zero = jnp.zeros_like(q)

    def softmax_v(qm):
        s = lax.dot_general(qm, k, (((1,), (1,)), ((), ())), preferred_element_type=F32)
        p = jnp.exp2(s - jnp.max(s, axis=-1, keepdims=True))
        ov = jnp.dot(p.astype(BF16), vext_ref[...], preferred_element_type=F32)
        return ov[:, :HEAD_DIM] / ov[:, HEAD_DIM:]

    o1 = softmax_v(jnp.where(lane < DIFF_DIM, q, zero))
    o2 = softmax_v(jnp.where(lane >= DIFF_DIM, q, zero))
    y = o1 - lam * o2
    y = _rms_scale(y, SUBLN_EPS) * g_ref[...] * (1.0 - LAM_INIT)
    o_ref[...] = y.astype(BF16)


def _attn_b(proj_b, lambda_qk, subln, batch, seq, tq=256):
    t = batch * seq
    q_tiles = seq // tq
    return pl.pallas_call(
        _attn_b_kernel,
        grid=(batch, N_HEADS, q_tiles),
        in_specs=[
            pl.BlockSpec((tq, HEAD_DIM), lambda b, h, i: (b * q_tiles + i, h)),
            pl.BlockSpec((seq, HEAD_DIM), lambda b, h, i: (b, N_HEADS + h)),
            pl.BlockSpec((seq, HEAD_DIM), lambda b, h, i: (b, 2 * N_HEADS + h)),
            pl.BlockSpec((4, DIFF_DIM), lambda b, h, i: (0, 0)),
            pl.BlockSpec((1, HEAD_DIM), lambda b, h, i: (0, 0)),
        ],
        out_specs=pl.BlockSpec((tq, HEAD_DIM), lambda b, h, i: (b * q_tiles + i, h)),
        out_shape=jax.ShapeDtypeStruct((t, GROUP_WIDTH), BF16),
        scratch_shapes=[pltpu.VMEM((seq, 2 * HEAD_DIM), BF16)],
        compiler_params=_params(("arbitrary", "arbitrary", "arbitrary")),
        name="attn_b",
    )(proj_b, proj_b, proj_b, lambda_qk, subln)


def _out_proj_kernel(ya_ref, yb_ref, wa_ref, wb_ref, x_ref, o_ref):
    acc = jnp.dot(ya_ref[...], wa_ref[...], preferred_element_type=F32)
    acc = acc + jnp.dot(yb_ref[...], wb_ref[...], preferred_element_type=F32)
    o_ref[...] = x_ref[...] + acc


def _out_proj(ya, yb, w_bf16, x2d, tm=1024, tn=1024):
    t = x2d.shape[0]
    k_blocks = GROUP_WIDTH // GROUP_WIDTH
    return pl.pallas_call(
        _out_proj_kernel,
        grid=(t // tm, D_MODEL // tn),
        in_specs=[
            pl.BlockSpec((tm, GROUP_WIDTH), lambda m, n: (m, 0)),
            pl.BlockSpec((tm, GROUP_WIDTH), lambda m, n: (m, 0)),
            pl.BlockSpec((GROUP_WIDTH, tn), lambda m, n: (0, n)),
            pl.BlockSpec((GROUP_WIDTH, tn), lambda m, n: (k_blocks, n)),
            pl.BlockSpec((tm, tn), lambda m, n: (m, n)),
        ],
        out_specs=pl.BlockSpec((tm, tn), lambda m, n: (m, n)),
        out_shape=jax.ShapeDtypeStruct((t, D_MODEL), F32),
        compiler_params=_params(("arbitrary", "arbitrary")),
        name="out_proj",
    )(ya, yb, w_bf16, w_bf16, x2d)


def _ffn_up_kernel(x_ref, g_ref, wg_ref, wu_ref, o_ref, h_ref):
    @pl.when(pl.program_id(1) == 0)
    def _():
        h_ref[...] = (_rms_scale(x_ref[...], RMS_EPS) * g_ref[...]).astype(BF16)

    h = h_ref[...]
    gate = jnp.dot(h, wg_ref[...], preferred_element_type=F32)
    up = jnp.dot(h, wu_ref[...], preferred_element_type=F32)
    o_ref[...] = (gate / (1.0 + jnp.exp(-gate)) * up).astype(BF16)


def _ffn_up(x2d, gain, wg_bf16, wu_bf16, tm=1024, tf=512):
    t = x2d.shape[0]
    return pl.pallas_call(
        _ffn_up_kernel,
        grid=(t // tm, D_FF // tf),
        in_specs=[
            pl.BlockSpec((tm, D_MODEL), lambda m, j: (m, 0)),
            pl.BlockSpec((1, D_MODEL), lambda m, j: (0, 0)),
            pl.BlockSpec((D_MODEL, tf), lambda m, j: (0, j)),
            pl.BlockSpec((D_MODEL, tf), lambda m, j: (0, j)),
        ],
        out_specs=pl.BlockSpec((tm, tf), lambda m, j: (m, j)),
        out_shape=jax.ShapeDtypeStruct((t, D_FF), BF16),
        scratch_shapes=[pltpu.VMEM((tm, D_MODEL), BF16)],
        compiler_params=_params(("arbitrary", "arbitrary")),
        name="ffn_up",
    )(x2d, gain, wg_bf16, wu_bf16)


def _ffn_down_kernel(a_ref, w_ref, x_ref, g_ref, o_ref, acc_ref):
    j = pl.program_id(1)

    @pl.when(j == 0)
    def _():
        acc_ref[...] = x_ref[...]

    acc_ref[...] += jnp.dot(a_ref[...], w_ref[...], preferred_element_type=F32)

    @pl.when(j == pl.num_programs(1) - 1)
    def _():
        o_ref[...] = _rms_scale(acc_ref[...], RMS_EPS) * g_ref[...]


def _ffn_down(act, w_bf16, x2d, gain, tm=512, tk=512):
    t = x2d.shape[0]
    return pl.pallas_call(
        _ffn_down_kernel,
        grid=(t // tm, D_FF // tk),
        in_specs=[
            pl.BlockSpec((tm, tk), lambda m, j: (m, j)),
            pl.BlockSpec((tk, D_MODEL), lambda m, j: (j, 0)),
            pl.BlockSpec((tm, D_MODEL), lambda m, j: (m, 0)),
            pl.BlockSpec((1, D_MODEL), lambda m, j: (0, 0)),
        ],
        out_specs=pl.BlockSpec((tm, D_MODEL), lambda m, j: (m, 0)),
        out_shape=jax.ShapeDtypeStruct((t, D_MODEL), F32),
        scratch_shapes=[pltpu.VMEM((tm, D_MODEL), F32)],
        compiler_params=_params(("arbitrary", "arbitrary")),
        name="ffn_down",
    )(act, w_bf16, x2d, gain)


def kernel(x, norm_attn, w_in, lambda_qk, subln, w_out, norm_ffn, w_gate, w_up, w_down, norm_final):
    batch, seq, d_model = x.shape
    assert d_model == D_MODEL and w_in.shape == (1, D_MODEL, 6 * GROUP_WIDTH)
    assert w_gate.shape == (1, D_MODEL, D_FF) and seq % 256 == 0
    x2d = x.reshape(batch * seq, D_MODEL)
    tabs = _rope_tables(seq)

    proj_a, proj_b = _in_proj(x2d, norm_attn[0][None, :], w_in[0].astype(BF16), tabs, seq)
    ya = _attn_a(proj_a, batch, seq)
    yb = _attn_b(proj_b, lambda_qk[0], subln[0][None, :], batch, seq)
    x1 = _out_proj(ya, yb, w_out[0].astype(BF16), x2d)
    act = _ffn_up(x1, norm_ffn[0][None, :], w_gate[0].astype(BF16), w_up[0].astype(BF16))
    out = _ffn_down(act, w_down[0].astype(BF16), x1, norm_final[None, :])
    return out.reshape(batch, seq, D_MODEL)
```

```python
import functools
import math

import jax
import jax.numpy as jnp
from jax import lax
from jax.experimental import pallas as pl
from jax.experimental.pallas import tpu as pltpu

D_MODEL = 2048
HEAD_DIM = 128
N_HEADS = 8
GROUP_WIDTH = N_HEADS * HEAD_DIM
DIFF_DIM = HEAD_DIM // 2
D_FF = 5632
ROPE_THETA = 500000.0
RMS_EPS = 1e-6
SUBLN_EPS = 1e-5
NEG_BIAS = -1e30
HALF_WINDOW = 64
DILATIONS = (1, 4, 16)
LAM_INIT = 0.8 - 0.6 * math.exp(-0.3 * 0)
LOG2E = math.log2(math.e)

VMEM_LIMIT_BYTES = 56 * 1024 * 1024
LANES = 128

BF16 = jnp.bfloat16
F32 = jnp.float32


def _params(semantics):
    return pltpu.CompilerParams(dimension_semantics=semantics, vmem_limit_bytes=VMEM_LIMIT_BYTES)


def _rms_scale(x, eps):
    return x * lax.rsqrt(jnp.mean(x * x, axis=-1, keepdims=True) + eps)


def _rope_tables(seq):
    pos = jnp.arange(seq, dtype=F32)[:, None]
    lane = jnp.arange(LANES)

    def tables(period, rot_dim, scale):
        half = rot_dim // 2
        inv_freq = ROPE_THETA ** (-jnp.arange(0, rot_dim, 2, dtype=F32) / rot_dim)
        ang = pos * inv_freq[None, :]
        cos, sin = jnp.cos(ang), jnp.sin(ang)
        j = lane % period
        idx = j % half
        in_lo = j < half
        in_hi = (j >= half) & (j < rot_dim)
        cos_l = jnp.take(cos, idx, axis=1)
        sin_l = jnp.take(sin, idx, axis=1)
        c = jnp.where((in_lo | in_hi)[None, :], cos_l, 1.0)
        s1 = jnp.where(in_lo[None, :], -sin_l, 0.0)
        s2 = jnp.where(in_hi[None, :], sin_l, 0.0)
        return jnp.stack([c, s1, s2]) * scale

    scale_a = HEAD_DIM ** -0.5 * LOG2E
    scale_b = DIFF_DIM ** -0.5 * LOG2E
    t = jnp.stack([
        tables(HEAD_DIM, HEAD_DIM // 4, scale_a),
        tables(HEAD_DIM, HEAD_DIM // 4, 1.0),
        tables(DIFF_DIM, DIFF_DIM // 4, scale_b),
        tables(DIFF_DIM, DIFF_DIM // 4, 1.0),
    ])
    return t[:, 0], t[:, 1], t[:, 2]


def _in_proj_kernel(x_ref, g_ref, w_ref, c_ref, s1_ref, s2_ref, oa_ref, ob_ref, h_ref):
    n = pl.program_id(1)

    @pl.when(n == 0)
    def _():
        h_ref[...] = (_rms_scale(x_ref[...], RMS_EPS) * g_ref[...]).astype(BF16)

    acc = jnp.dot(h_ref[...], w_ref[...], preferred_element_type=F32)

    def rope(shift):
        c, s1, s2 = c_ref[0], s1_ref[0], s2_ref[0]
        heads = []
        for h in range(N_HEADS):
            a = acc[:, h * HEAD_DIM:(h + 1) * HEAD_DIM]
            heads.append(a * c
                         + pltpu.roll(a, HEAD_DIM - shift, 1) * s1
                         + pltpu.roll(a, shift, 1) * s2)
        return jnp.concatenate(heads, axis=1)

    @pl.when(n < 2)
    def _():
        oa_ref[...] = rope(HEAD_DIM // 8)

    @pl.when(n == 2)
    def _():
        oa_ref[...] = acc

    @pl.when((n == 3) | (n == 4))
    def _():
        ob_ref[...] = rope(DIFF_DIM // 8).astype(BF16)

    @pl.when(n == 5)
    def _():
        ob_ref[...] = acc.astype(BF16)


def _in_proj(x2d, gain, w_bf16, tabs, seq, tm=1024):
    t = x2d.shape[0]
    c_t, s1_t, s2_t = tabs
    seq_tiles = seq // tm
    tab_idx = lambda m, n: (jnp.where(n < 2, n, jnp.clip(n - 1, 2, 3)), m % seq_tiles, 0)
    tab_spec = pl.BlockSpec((1, tm, LANES), tab_idx)
    return pl.pallas_call(
        _in_proj_kernel,
        grid=(t // tm, 6),
        in_specs=[
            pl.BlockSpec((tm, D_MODEL), lambda m, n: (m, 0)),
            pl.BlockSpec((1, D_MODEL), lambda m, n: (0, 0)),
            pl.BlockSpec((D_MODEL, GROUP_WIDTH), lambda m, n: (0, n)),
            tab_spec, tab_spec, tab_spec,
        ],
        out_specs=[
            pl.BlockSpec((tm, GROUP_WIDTH), lambda m, n: (m, jnp.minimum(n, 2))),
            pl.BlockSpec((tm, GROUP_WIDTH), lambda m, n: (m, jnp.maximum(n - 3, 0))),
        ],
        out_shape=[
            jax.ShapeDtypeStruct((t, 3 * GROUP_WIDTH), F32),
            jax.ShapeDtypeStruct((t, 3 * GROUP_WIDTH), BF16),
        ],
        scratch_shapes=[pltpu.VMEM((tm, D_MODEL), BF16)],
        compiler_params=_params(("arbitrary", "arbitrary")),
        name="in_proj",
    )(x2d, gain, w_bf16, c_t, s1_t, s2_t)


def _window_bias(n_q, n_k, offset):
    row = lax.broadcasted_iota(jnp.int32, (n_q, n_k), 0)
    col = lax.broadcasted_iota(jnp.int32, (n_q, n_k), 1)
    dist = col + offset - row
    return jnp.where(jnp.abs(dist) <= HALF_WINDOW, 0.0, NEG_BIAS).astype(F32)


def _attn_a_kernel(q_ref, k_ref, v_ref, o_ref, qs_ref, ks_ref, vs_ref, ob_ref, lb_ref, *, seq):
    tq = 128
    for g, dil in enumerate(DILATIONS):
        length = seq // dil
        kw = min(256, length)
        for r in range(dil):
            rows = pl.ds(r, length, stride=dil) if dil > 1 else pl.ds(0, length)
            dst = pl.ds(r * length, length)
            qs_ref[dst, :] = q_ref[rows, :].astype(BF16)
            ks_ref[dst, :] = k_ref[rows, :].astype(BF16)
            vs_ref[dst, :] = v_ref[rows, :].astype(BF16)
        for r in range(dil):
            for tile in range(length // tq):
                q0 = tile * tq
                ws = min(max(q0 - HALF_WINDOW, 0), length - kw)
                base = r * length
                qt = qs_ref[pl.ds(base + q0, tq), :]
                kt = ks_ref[pl.ds(base + ws, kw), :]
                vt = vs_ref[pl.ds(base + ws, kw), :]
                s = lax.dot_general(qt, kt, (((1,), (1,)), ((), ())), preferred_element_type=F32)
                s = s + _window_bias(tq, kw, ws - q0)
                m = jnp.max(s, axis=-1, keepdims=True)
                p = jnp.exp2(s - m)
                den = jnp.sum(p, axis=-1, keepdims=True)
                o = jnp.dot(p.astype(BF16), vt, preferred_element_type=F32) / den
                lse = m + jnp.log2(den)
                out_rows = pl.ds(q0 * dil + r, tq, stride=dil) if dil > 1 else pl.ds(q0, tq)
                ob_ref[g, out_rows, :] = o
                lb_ref[g, out_rows, :] = jnp.broadcast_to(lse, (tq, LANES))
    l0, l1, l2 = lb_ref[0], lb_ref[1], lb_ref[2]
    mx = jnp.maximum(jnp.maximum(l0, l1), l2)
    e0, e1, e2 = jnp.exp2(l0 - mx), jnp.exp2(l1 - mx), jnp.exp2(l2 - mx)
    merged = (e0 * ob_ref[0] + e1 * ob_ref[1] + e2 * ob_ref[2]) / (e0 + e1 + e2)
    o_ref[...] = merged.astype(BF16)


def _attn_a(proj_a, batch, seq):
    t = batch * seq
    spec = lambda col0: pl.BlockSpec((seq, HEAD_DIM), lambda b, h: (b, col0 + h))
    return pl.pallas_call(
        functools.partial(_attn_a_kernel, seq=seq),
        grid=(batch, N_HEADS),
        in_specs=[spec(0), spec(N_HEADS), spec(2 * N_HEADS)],
        out_specs=pl.BlockSpec((seq, HEAD_DIM), lambda b, h: (b, h)),
        out_shape=jax.ShapeDtypeStruct((t, GROUP_WIDTH), BF16),
        scratch_shapes=[
            pltpu.VMEM((seq, HEAD_DIM), BF16),
            pltpu.VMEM((seq, HEAD_DIM), BF16),
            pltpu.VMEM((seq, HEAD_DIM), BF16),
            pltpu.VMEM((len(DILATIONS), seq, HEAD_DIM), F32),
            pltpu.VMEM((len(DILATIONS), seq, LANES), F32),
        ],
        compiler_params=_params(("arbitrary", "arbitrary")),
        name="attn_a",
    )(proj_a, proj_a, proj_a)


ONES_ROWS = 16


def _attn_b_kernel(q_ref, k_ref, v_ref, lq_ref, g_ref, o_ref, vt_ref, *, tq):
    seq = q_ref.shape[0]
    vt_ref[:HEAD_DIM, :] = v_ref[...].astype(F32).T.astype(BF16)
    vt_ref[HEAD_DIM:, :] = jnp.ones((ONES_ROWS, seq), BF16)

    lq = lq_ref[...]
    lam = (jnp.exp(jnp.sum(lq[0:1] * lq[1:2], axis=-1, keepdims=True))
           - jnp.exp(jnp.sum(lq[2:3] * lq[3:4], axis=-1, keepdims=True)) + LAM_INIT)

    k = k_ref[...]
    lane = lax.broadcasted_iota(jnp.int32, (tq, HEAD_DIM), 1)
    zero = jnp.zeros((tq, HEAD_DIM), BF16)

    def scores_t(qm):
        return lax.dot_general(k, qm, (((1,), (1,)), ((), ())), preferred_element_type=F32)

    def probs_t(st):
        return jnp.exp2(st - jnp.max(st, axis=0, keepdims=True)).astype(BF16)

    def weighted_v(pt):
        ov = jnp.dot(vt_ref[...], pt, preferred_element_type=F32)
        return ov[:HEAD_DIM] / ov[HEAD_DIM:HEAD_DIM + 1]

    n_tiles = seq // tq
    scores, probs = {}, {}
    for i in range(n_tiles + 2):
        if i < n_tiles:
            q = q_ref[i * tq:(i + 1) * tq, :]
            scores[i] = (scores_t(jnp.where(lane < DIFF_DIM, q, zero)),
                         scores_t(jnp.where(lane >= DIFF_DIM, q, zero)))
        if 1 <= i <= n_tiles:
            st1, st2 = scores.pop(i - 1)
            probs[i - 1] = (probs_t(st1), probs_t(st2))
        if i >= 2:
            pt1, pt2 = probs.pop(i - 2)
            yt = weighted_v(pt1) - lam * weighted_v(pt2)
            yt = yt * lax.rsqrt(jnp.mean(yt * yt, axis=0, keepdims=True) + SUBLN_EPS)
            yt = yt * g_ref[...] * (1.0 - LAM_INIT)
            o_ref[(i - 2) * tq:(i - 1) * tq, :] = yt.T.astype(BF16)


def _attn_b(proj_b, lambda_qk, subln, batch, seq, tq=256):
    t = batch * seq
    return pl.pallas_call(
        functools.partial(_attn_b_kernel, tq=tq),
        grid=(batch, N_HEADS),
        in_specs=[
            pl.BlockSpec((seq, HEAD_DIM), lambda b, h: (b, h)),
            pl.BlockSpec((seq, HEAD_DIM), lambda b, h: (b, N_HEADS + h)),
            pl.BlockSpec((seq, HEAD_DIM), lambda b, h: (b, 2 * N_HEADS + h)),
            pl.BlockSpec((4, DIFF_DIM), lambda b, h: (0, 0)),
            pl.BlockSpec((HEAD_DIM, 1), lambda b, h: (0, 0)),
        ],
        out_specs=pl.BlockSpec((seq, HEAD_DIM), lambda b, h: (b, h)),
        out_shape=jax.ShapeDtypeStruct((t, GROUP_WIDTH), BF16),
        scratch_shapes=[pltpu.VMEM((HEAD_DIM + ONES_ROWS, seq), BF16)],
        compiler_params=_params(("arbitrary", "arbitrary")),
        name="attn_b",
    )(proj_b, proj_b, proj_b, lambda_qk, subln)


def _out_proj_kernel(ya_ref, yb_ref, wa_ref, wb_ref, x_ref, o_ref):
    acc = jnp.dot(ya_ref[...], wa_ref[...], preferred_element_type=F32)
    acc = acc + jnp.dot(yb_ref[...], wb_ref[...], preferred_element_type=F32)
    o_ref[...] = x_ref[...] + acc


def _out_proj(ya, yb, w_bf16, x2d, tm=1024, tn=1024):
    t = x2d.shape[0]
    k_blocks = GROUP_WIDTH // GROUP_WIDTH
    return pl.pallas_call(
        _out_proj_kernel,
        grid=(t // tm, D_MODEL // tn),
        in_specs=[
            pl.BlockSpec((tm, GROUP_WIDTH), lambda m, n: (m, 0)),
            pl.BlockSpec((tm, GROUP_WIDTH), lambda m, n: (m, 0)),
            pl.BlockSpec((GROUP_WIDTH, tn), lambda m, n: (0, n)),
            pl.BlockSpec((GROUP_WIDTH, tn), lambda m, n: (k_blocks, n)),
            pl.BlockSpec((tm, tn), lambda m, n: (m, n)),
        ],
        out_specs=pl.BlockSpec((tm, tn), lambda m, n: (m, n)),
        out_shape=jax.ShapeDtypeStruct((t, D_MODEL), F32),
        compiler_params=_params(("arbitrary", "arbitrary")),
        name="out_proj",
    )(ya, yb, w_bf16, w_bf16, x2d)


def _ffn_up_kernel(x_ref, g_ref, wg_ref, wu_ref, o_ref, h_ref):
    @pl.when(pl.program_id(1) == 0)
    def _():
        h_ref[...] = (_rms_scale(x_ref[...], RMS_EPS) * g_ref[...]).astype(BF16)

    h = h_ref[...]
    gate = jnp.dot(h, wg_ref[...], preferred_element_type=F32)
    up = jnp.dot(h, wu_ref[...], preferred_element_type=F32)
    o_ref[...] = (gate / (1.0 + jnp.exp(-gate)) * up).astype(BF16)


def _ffn_up(x2d, gain, wg_bf16, wu_bf16, tm=1024, tf=512):
    t = x2d.shape[0]
    return pl.pallas_call(
        _ffn_up_kernel,
        grid=(t // tm, D_FF // tf),
        in_specs=[
            pl.BlockSpec((tm, D_MODEL), lambda m, j: (m, 0)),
            pl.BlockSpec((1, D_MODEL), lambda m, j: (0, 0)),
            pl.BlockSpec((D_MODEL, tf), lambda m, j: (0, j)),
            pl.BlockSpec((D_MODEL, tf), lambda m, j: (0, j)),
        ],
        out_specs=pl.BlockSpec((tm, tf), lambda m, j: (m, j)),
        out_shape=jax.ShapeDtypeStruct((t, D_FF), BF16),
        scratch_shapes=[pltpu.VMEM((tm, D_MODEL), BF16)],
        compiler_params=_params(("arbitrary", "arbitrary")),
        name="ffn_up",
    )(x2d, gain, wg_bf16, wu_bf16)


def _ffn_down_kernel(a_ref, w_ref, x_ref, g_ref, o_ref, acc_ref):
    j = pl.program_id(1)

    @pl.when(j == 0)
    def _():
        acc_ref[...] = x_ref[...]

    acc_ref[...] += jnp.dot(a_ref[...], w_ref[...], preferred_element_type=F32)

    @pl.when(j == pl.num_programs(1) - 1)
    def _():
        o_ref[...] = _rms_scale(acc_ref[...], RMS_EPS) * g_ref[...]


def _ffn_down(act, w_bf16, x2d, gain, tm=512, tk=512):
    t = x2d.shape[0]
    return pl.pallas_call(
        _ffn_down_kernel,
        grid=(t // tm, D_FF // tk),
        in_specs=[
            pl.BlockSpec((tm, tk), lambda m, j: (m, j)),
            pl.BlockSpec((tk, D_MODEL), lambda m, j: (j, 0)),
            pl.BlockSpec((tm, D_MODEL), lambda m, j: (m, 0)),
            pl.BlockSpec((1, D_MODEL), lambda m, j: (0, 0)),
        ],
        out_specs=pl.BlockSpec((tm, D_MODEL), lambda m, j: (m, 0)),
        out_shape=jax.ShapeDtypeStruct((t, D_MODEL), F32),
        scratch_shapes=[pltpu.VMEM((tm, D_MODEL), F32)],
        compiler_params=_params(("arbitrary", "arbitrary")),
        name="ffn_down",
    )(act, w_bf16, x2d, gain)


def kernel(x, norm_attn, w_in, lambda_qk, subln, w_out, norm_ffn, w_gate, w_up, w_down, norm_final):
    batch, seq, d_model = x.shape
    assert d_model == D_MODEL and w_in.shape == (1, D_MODEL, 6 * GROUP_WIDTH)
    assert w_gate.shape == (1, D_MODEL, D_FF) and seq % 256 == 0
    x2d = x.reshape(batch * seq, D_MODEL)
    tabs = _rope_tables(seq)

    proj_a, proj_b = _in_proj(x2d, norm_attn[0][None, :], w_in[0].astype(BF16), tabs, seq)
    ya = _attn_a(proj_a, batch, seq)
    yb = _attn_b(proj_b, lambda_qk[0], subln[0][:, None], batch, seq)
    x1 = _out_proj(ya, yb, w_out[0].astype(BF16), x2d)
    act = _ffn_up(x1, norm_ffn[0][None, :], w_gate[0].astype(BF16), w_up[0].astype(BF16))
    out = _ffn_down(act, w_down[0].astype(BF16), x1, norm_final[None, :])
    return out.reshape(batch, seq, D_MODEL)
```

```python
import functools
import math

import jax
import jax.numpy as jnp
from jax import lax
from jax.experimental import pallas as pl
from jax.experimental.pallas import tpu as pltpu

D_MODEL = 2048
HEAD_DIM = 128
N_HEADS = 8
GROUP_WIDTH = N_HEADS * HEAD_DIM
DIFF_DIM = HEAD_DIM // 2
D_FF = 5632
ROPE_THETA = 500000.0
RMS_EPS = 1e-6
SUBLN_EPS = 1e-5
NEG_BIAS = -1e30
HALF_WINDOW = 64
DILATIONS = (1, 4, 16)
LAM_INIT = 0.8 - 0.6 * math.exp(-0.3 * 0)
LOG2E = math.log2(math.e)

VMEM_LIMIT_BYTES = 56 * 1024 * 1024
LANES = 128

BF16 = jnp.bfloat16
F32 = jnp.float32


def _params(semantics):
    return pltpu.CompilerParams(dimension_semantics=semantics, vmem_limit_bytes=VMEM_LIMIT_BYTES)


def _rms_scale(x, eps):
    return x * lax.rsqrt(jnp.mean(x * x, axis=-1, keepdims=True) + eps)


def _rope_tables(seq):
    pos = jnp.arange(seq, dtype=F32)[:, None]
    lane = jnp.arange(LANES)

    def tables(period, rot_dim, scale):
        half = rot_dim // 2
        inv_freq = ROPE_THETA ** (-jnp.arange(0, rot_dim, 2, dtype=F32) / rot_dim)
        ang = pos * inv_freq[None, :]
        cos, sin = jnp.cos(ang), jnp.sin(ang)
        j = lane % period
        idx = j % half
        in_lo = j < half
        in_hi = (j >= half) & (j < rot_dim)
        cos_l = jnp.take(cos, idx, axis=1)
        sin_l = jnp.take(sin, idx, axis=1)
        c = jnp.where((in_lo | in_hi)[None, :], cos_l, 1.0)
        s1 = jnp.where(in_lo[None, :], -sin_l, 0.0)
        s2 = jnp.where(in_hi[None, :], sin_l, 0.0)
        return jnp.stack([c, s1, s2]) * scale

    scale_a = HEAD_DIM ** -0.5 * LOG2E
    scale_b = DIFF_DIM ** -0.5 * LOG2E
    t = jnp.stack([
        tables(HEAD_DIM, HEAD_DIM // 4, scale_a),
        tables(HEAD_DIM, HEAD_DIM // 4, 1.0),
        tables(DIFF_DIM, DIFF_DIM // 4, scale_b),
        tables(DIFF_DIM, DIFF_DIM // 4, 1.0),
    ])
    return t[:, 0], t[:, 1], t[:, 2]


def _in_proj_kernel(x_ref, g_ref, w_ref, c_ref, s1_ref, s2_ref, oa_ref, ob_ref, h_ref, *, rows):
    n = pl.program_id(1)
    n_chunks = x_ref.shape[0] // rows

    def chunk(c):
        return pl.ds(c * rows, rows)

    def norm(c):
        h_ref[chunk(c), :] = (_rms_scale(x_ref[chunk(c), :], RMS_EPS) * g_ref[...]).astype(BF16)

    def matmul(c):
        return jnp.dot(h_ref[chunk(c), :], w_ref[...], preferred_element_type=F32)

    def rope_store(out_ref, shift):
        def store(c, acc):
            cos, s1, s2 = c_ref[0, chunk(c), :], s1_ref[0, chunk(c), :], s2_ref[0, chunk(c), :]
            for h in range(N_HEADS):
                cols = slice(h * HEAD_DIM, (h + 1) * HEAD_DIM)
                a = acc[:, cols]
                y = a * cos + pltpu.roll(a, HEAD_DIM - shift, 1) * s1 + pltpu.roll(a, shift, 1) * s2
                out_ref[chunk(c), cols] = y.astype(out_ref.dtype)
        return store

    def plain_store(out_ref):
        def store(c, acc):
            out_ref[chunk(c), :] = acc.astype(out_ref.dtype)
        return store

    def run(store, with_norm=False):
        accs = {}
        for i in range(n_chunks + 2):
            if with_norm and i < n_chunks:
                norm(i)
            if 1 <= i <= n_chunks:
                accs[i - 1] = matmul(i - 1)
            if i >= 2:
                store(i - 2, accs.pop(i - 2))

    pl.when(n == 0)(lambda: run(rope_store(oa_ref, HEAD_DIM // 8), with_norm=True))
    pl.when(n == 1)(lambda: run(rope_store(oa_ref, HEAD_DIM // 8)))
    pl.when(n == 2)(lambda: run(plain_store(oa_ref)))
    pl.when((n == 3) | (n == 4))(lambda: run(rope_store(ob_ref, DIFF_DIM // 8)))
    pl.when(n == 5)(lambda: run(plain_store(ob_ref)))


def _in_proj(x2d, gain, w_bf16, tabs, seq, tm=1024, rows=256):
    t = x2d.shape[0]
    c_t, s1_t, s2_t = tabs
    seq_tiles = seq // tm
    tab_idx = lambda m, n: (jnp.where(n < 2, n, jnp.clip(n - 1, 2, 3)), m % seq_tiles, 0)
    tab_spec = pl.BlockSpec((1, tm, LANES), tab_idx)
    return pl.pallas_call(
        functools.partial(_in_proj_kernel, rows=rows),
        grid=(t // tm, 6),
        in_specs=[
            pl.BlockSpec((tm, D_MODEL), lambda m, n: (m, 0)),
            pl.BlockSpec((1, D_MODEL), lambda m, n: (0, 0)),
            pl.BlockSpec((D_MODEL, GROUP_WIDTH), lambda m, n: (0, n)),
            tab_spec, tab_spec, tab_spec,
        ],
        out_specs=[
            pl.BlockSpec((tm, GROUP_WIDTH), lambda m, n: (m, jnp.minimum(n, 2))),
            pl.BlockSpec((tm, GROUP_WIDTH), lambda m, n: (m, jnp.maximum(n - 3, 0))),
        ],
        out_shape=[
            jax.ShapeDtypeStruct((t, 3 * GROUP_WIDTH), F32),
            jax.ShapeDtypeStruct((t, 3 * GROUP_WIDTH), BF16),
        ],
        scratch_shapes=[pltpu.VMEM((tm, D_MODEL), BF16)],
        compiler_params=_params(("arbitrary", "arbitrary")),
        name="in_proj",
    )(x2d, gain, w_bf16, c_t, s1_t, s2_t)


def _window_bias(n_q, n_k, offset):
    row = lax.broadcasted_iota(jnp.int32, (n_q, n_k), 0)
    col = lax.broadcasted_iota(jnp.int32, (n_q, n_k), 1)
    dist = col + offset - row
    return jnp.where(jnp.abs(dist) <= HALF_WINDOW, 0.0, NEG_BIAS).astype(F32)


def _attn_a_kernel(q_ref, k_ref, v_ref, o_ref, st_ref, qs_ref, ks_ref, vs_ref, ob_ref, lb_ref, *, seq):
    tq = 128
    srcs = (q_ref, k_ref, v_ref)
    dsts = (qs_ref, ks_ref, vs_ref)
    n_dil = len(DILATIONS)
    assert DILATIONS == (1, 4, 16)
    len4, len16 = seq // 4, seq // 16

    for a in range(3):
        dsts[a][0, :, :HEAD_DIM] = srcs[a][...].astype(BF16)
        for r4 in range(4):
            part = srcs[a][pl.ds(r4, len4, stride=4), :]
            st_ref[a, pl.ds(r4 * len4, len4), :] = part
            dsts[a][1, pl.ds(r4 * len4, len4), :HEAD_DIM] = part.astype(BF16)
        for r4 in range(4):
            for j in range(4):
                part = st_ref[a, pl.ds(r4 * len4 + j, len16, stride=4), :]
                dsts[a][2, pl.ds((r4 + 4 * j) * len16, len16), :HEAD_DIM] = part.astype(BF16)
    vs_ref[:, :, HEAD_DIM:] = jnp.ones((n_dil, seq, HEAD_DIM), BF16)

    tiles = []
    for g, dil in enumerate(DILATIONS):
        length = seq // dil
        kw = min(256, length)
        for r in range(dil):
            for tile in range(length // tq):
                q0 = tile * tq
                ws = min(max(q0 - HALF_WINDOW, 0), length - kw)
                tiles.append((g, dil, r, r * length, q0, ws, kw))

    def scores(t):
        g, dil, r, base, q0, ws, kw = t
        qt = qs_ref[g, pl.ds(base + q0, tq), :HEAD_DIM]
        kt = ks_ref[g, pl.ds(base + ws, kw), :HEAD_DIM]
        s = lax.dot_general(qt, kt, (((1,), (1,)), ((), ())), preferred_element_type=F32)
        return s + _window_bias(tq, kw, ws - q0)

    def softmax(s):
        m = jnp.max(s, axis=-1, keepdims=True)
        return jnp.exp2(s - m).astype(BF16), m

    def finish(t, p, m):
        g, dil, r, base, q0, ws, kw = t
        ov = jnp.dot(p, vs_ref[g, pl.ds(base + ws, kw), :], preferred_element_type=F32)
        den = ov[:, HEAD_DIM:]
        out_rows = pl.ds(q0 * dil + r, tq, stride=dil) if dil > 1 else pl.ds(q0, tq)
        ob_ref[g, out_rows, :] = ov[:, :HEAD_DIM] / den
        lb_ref[g, out_rows, :] = m + jnp.log2(den)

    s_vals, p_vals = {}, {}
    for i in range(len(tiles) + 2):
        if i < len(tiles):
            s_vals[i] = scores(tiles[i])
        if 1 <= i <= len(tiles):
            p_vals[i - 1] = softmax(s_vals.pop(i - 1))
        if i >= 2:
            finish(tiles[i - 2], *p_vals.pop(i - 2))

    l0, l1, l2 = lb_ref[0], lb_ref[1], lb_ref[2]
    mx = jnp.maximum(jnp.maximum(l0, l1), l2)
    e0, e1, e2 = jnp.exp2(l0 - mx), jnp.exp2(l1 - mx), jnp.exp2(l2 - mx)
    merged = (e0 * ob_ref[0] + e1 * ob_ref[1] + e2 * ob_ref[2]) / (e0 + e1 + e2)
    o_ref[...] = merged.astype(BF16)


def _attn_a(proj_a, batch, seq):
    t = batch * seq
    spec = lambda col0: pl.BlockSpec((seq, HEAD_DIM), lambda b, h: (b, col0 + h))
    return pl.pallas_call(
        functools.partial(_attn_a_kernel, seq=seq),
        grid=(batch, N_HEADS),
        in_specs=[spec(0), spec(N_HEADS), spec(2 * N_HEADS)],
        out_specs=pl.BlockSpec((seq, HEAD_DIM), lambda b, h: (b, h)),
        out_shape=jax.ShapeDtypeStruct((t, GROUP_WIDTH), BF16),
        scratch_shapes=[
            pltpu.VMEM((3, seq, HEAD_DIM), F32),
            pltpu.VMEM((len(DILATIONS), seq, HEAD_DIM), BF16),
            pltpu.VMEM((len(DILATIONS), seq, HEAD_DIM), BF16),
            pltpu.VMEM((len(DILATIONS), seq, 2 * HEAD_DIM), BF16),
            pltpu.VMEM((len(DILATIONS), seq, HEAD_DIM), F32),
            pltpu.VMEM((len(DILATIONS), seq, LANES), F32),
        ],
        compiler_params=_params(("arbitrary", "arbitrary")),
        name="attn_a",
    )(proj_a, proj_a, proj_a)


ONES_ROWS = 16


def _attn_b_kernel(q_ref, k_ref, v_ref, lq_ref, g_ref, o_ref, vt_ref, *, tq):
    seq = q_ref.shape[0]
    vt_ref[:HEAD_DIM, :] = v_ref[...].astype(F32).T.astype(BF16)
    vt_ref[HEAD_DIM:, :] = jnp.ones((ONES_ROWS, seq), BF16)

    lq = lq_ref[...]
    lam = (jnp.exp(jnp.sum(lq[0:1] * lq[1:2], axis=-1, keepdims=True))
           - jnp.exp(jnp.sum(lq[2:3] * lq[3:4], axis=-1, keepdims=True)) + LAM_INIT)

    k = k_ref[...]
    lane = lax.broadcasted_iota(jnp.int32, (tq, HEAD_DIM), 1)
    zero = jnp.zeros((tq, HEAD_DIM), BF16)

    def scores_t(qm):
        return lax.dot_general(k, qm, (((1,), (1,)), ((), ())), preferred_element_type=F32)

    def probs_t(st):
        return jnp.exp2(st - jnp.max(st, axis=0, keepdims=True)).astype(BF16)

    def weighted_v(pt):
        ov = jnp.dot(vt_ref[...], pt, preferred_element_type=F32)
        return ov[:HEAD_DIM] / ov[HEAD_DIM:HEAD_DIM + 1]

    n_tiles = seq // tq
    scores, probs = {}, {}
    for i in range(n_tiles + 2):
        if i < n_tiles:
            q = q_ref[i * tq:(i + 1) * tq, :]
            scores[i] = (scores_t(jnp.where(lane < DIFF_DIM, q, zero)),
                         scores_t(jnp.where(lane >= DIFF_DIM, q, zero)))
        if 1 <= i <= n_tiles:
            st1, st2 = scores.pop(i - 1)
            probs[i - 1] = (probs_t(st1), probs_t(st2))
        if i >= 2:
            pt1, pt2 = probs.pop(i - 2)
            yt = weighted_v(pt1) - lam * weighted_v(pt2)
            yt = yt * lax.rsqrt(jnp.mean(yt * yt, axis=0, keepdims=True) + SUBLN_EPS)
            yt = yt * g_ref[...] * (1.0 - LAM_INIT)
            o_ref[(i - 2) * tq:(i - 1) * tq, :] = yt.T.astype(BF16)


def _attn_b(proj_b, lambda_qk, subln, batch, seq, tq=256):
    t = batch * seq
    return pl.pallas_call(
        functools.partial(_attn_b_kernel, tq=tq),
        grid=(batch, N_HEADS),
        in_specs=[
            pl.BlockSpec((seq, HEAD_DIM), lambda b, h: (b, h)),
            pl.BlockSpec((seq, HEAD_DIM), lambda b, h: (b, N_HEADS + h)),
            pl.BlockSpec((seq, HEAD_DIM), lambda b, h: (b, 2 * N_HEADS + h)),
            pl.BlockSpec((4, DIFF_DIM), lambda b, h: (0, 0)),
            pl.BlockSpec((HEAD_DIM, 1), lambda b, h: (0, 0)),
        ],
        out_specs=pl.BlockSpec((seq, HEAD_DIM), lambda b, h: (b, h)),
        out_shape=jax.ShapeDtypeStruct((t, GROUP_WIDTH), BF16),
        scratch_shapes=[pltpu.VMEM((HEAD_DIM + ONES_ROWS, seq), BF16)],
        compiler_params=_params(("arbitrary", "arbitrary")),
        name="attn_b",
    )(proj_b, proj_b, proj_b, lambda_qk, subln)


def _out_proj_kernel(ya_ref, yb_ref, wa_ref, wb_ref, x_ref, o_ref):
    acc = jnp.dot(ya_ref[...], wa_ref[...], preferred_element_type=F32)
    acc = acc + jnp.dot(yb_ref[...], wb_ref[...], preferred_element_type=F32)
    o_ref[...] = x_ref[...] + acc


def _out_proj(ya, yb, w_bf16, x2d, tm=1024, tn=1024):
    t = x2d.shape[0]
    k_blocks = GROUP_WIDTH // GROUP_WIDTH
    return pl.pallas_call(
        _out_proj_kernel,
        grid=(t // tm, D_MODEL // tn),
        in_specs=[
            pl.BlockSpec((tm, GROUP_WIDTH), lambda m, n: (m, 0)),
            pl.BlockSpec((tm, GROUP_WIDTH), lambda m, n: (m, 0)),
            pl.BlockSpec((GROUP_WIDTH, tn), lambda m, n: (0, n)),
            pl.BlockSpec((GROUP_WIDTH, tn), lambda m, n: (k_blocks, n)),
            pl.BlockSpec((tm, tn), lambda m, n: (m, n)),
        ],
        out_specs=pl.BlockSpec((tm, tn), lambda m, n: (m, n)),
        out_shape=jax.ShapeDtypeStruct((t, D_MODEL), F32),
        compiler_params=_params(("arbitrary", "arbitrary")),
        name="out_proj",
    )(ya, yb, w_bf16, w_bf16, x2d)


def _ffn_kernel(x_ref, gn_ref, wg_ref, wu_ref, wd_ref, gf_ref, o_ref, h_ref, *, rows):
    j = pl.program_id(1)

    @pl.when(j == 0)
    def _():
        x = x_ref[...]
        h_ref[...] = (_rms_scale(x, RMS_EPS) * gn_ref[...]).astype(BF16)
        o_ref[...] = x

    n_chunks = x_ref.shape[0] // rows
    act = None
    for c in range(n_chunks + 1):
        if c < n_chunks:
            h = h_ref[c * rows:(c + 1) * rows, :]
            gate = jnp.dot(h, wg_ref[...], preferred_element_type=F32)
            up = jnp.dot(h, wu_ref[...], preferred_element_type=F32)
            new_act = (gate / (1.0 + jnp.exp(-gate)) * up).astype(BF16)
        if c >= 1:
            dst = pl.ds((c - 1) * rows, rows)
            o_ref[dst, :] += jnp.dot(act, wd_ref[...], preferred_element_type=F32)
        act = new_act

    @pl.when(j == pl.num_programs(1) - 1)
    def _():
        o_ref[...] = _rms_scale(o_ref[...], RMS_EPS) * gf_ref[...]


def _ffn(x2d, gain, wg_bf16, wu_bf16, wd_bf16, gain_final, tm=1024, tf=512, rows=256):
    t = x2d.shape[0]
    return pl.pallas_call(
        functools.partial(_ffn_kernel, rows=rows),
        grid=(t // tm, D_FF // tf),
        in_specs=[
            pl.BlockSpec((tm, D_MODEL), lambda m, j: (m, 0)),
            pl.BlockSpec((1, D_MODEL), lambda m, j: (0, 0)),
            pl.BlockSpec((D_MODEL, tf), lambda m, j: (0, j)),
            pl.BlockSpec((D_MODEL, tf), lambda m, j: (0, j)),
            pl.BlockSpec((tf, D_MODEL), lambda m, j: (j, 0)),
            pl.BlockSpec((1, D_MODEL), lambda m, j: (0, 0)),
        ],
        out_specs=pl.BlockSpec((tm, D_MODEL), lambda m, j: (m, 0)),
        out_shape=jax.ShapeDtypeStruct((t, D_MODEL), F32),
        scratch_shapes=[pltpu.VMEM((tm, D_MODEL), BF16)],
        compiler_params=_params(("arbitrary", "arbitrary")),
        name="ffn",
    )(x2d, gain, wg_bf16, wu_bf16, wd_bf16, gain_final)


def kernel(x, norm_attn, w_in, lambda_qk, subln, w_out, norm_ffn, w_gate, w_up, w_down, norm_final):
    batch, seq, d_model = x.shape
    assert d_model == D_MODEL and w_in.shape == (1, D_MODEL, 6 * GROUP_WIDTH)
    assert w_gate.shape == (1, D_MODEL, D_FF) and seq % 256 == 0
    x2d = x.reshape(batch * seq, D_MODEL)
    tabs = _rope_tables(seq)

    proj_a, proj_b = _in_proj(x2d, norm_attn[0][None, :], w_in[0].astype(BF16), tabs, seq)
    ya = _attn_a(proj_a, batch, seq)
    yb = _attn_b(proj_b, lambda_qk[0], subln[0][:, None], batch, seq)
    x1 = _out_proj(ya, yb, w_out[0].astype(BF16), x2d)
    out = _ffn(x1, norm_ffn[0][None, :], w_gate[0].astype(BF16), w_up[0].astype(BF16),
               w_down[0].astype(BF16), norm_final[None, :])
    return out.reshape(batch, seq, D_MODEL)
```

```python
import functools
import math

import jax
import jax.numpy as jnp
from jax import lax
from jax.experimental import pallas as pl
from jax.experimental.pallas import tpu as pltpu

D_MODEL = 2048
HEAD_DIM = 128
N_HEADS = 8
GROUP_WIDTH = N_HEADS * HEAD_DIM
DIFF_DIM = HEAD_DIM // 2
D_FF = 5632
ROPE_THETA = 500000.0
RMS_EPS = 1e-6
SUBLN_EPS = 1e-5
NEG_BIAS = -1e30
HALF_WINDOW = 64
DILATIONS = (1, 4, 16)
LAM_INIT = 0.8 - 0.6 * math.exp(-0.3 * 0)
LOG2E = math.log2(math.e)

VMEM_LIMIT_BYTES = 56 * 1024 * 1024
LANES = 128

BF16 = jnp.bfloat16
F32 = jnp.float32


def _params(semantics):
    return pltpu.CompilerParams(dimension_semantics=semantics, vmem_limit_bytes=VMEM_LIMIT_BYTES)


def _rms_scale(x, eps):
    return x * lax.rsqrt(jnp.mean(x * x, axis=-1, keepdims=True) + eps)


def _rope_tables(seq):
    pos = jnp.arange(seq, dtype=F32)[:, None]
    lane = jnp.arange(LANES)

    def tables(period, rot_dim, scale):
        half = rot_dim // 2
        inv_freq = ROPE_THETA ** (-jnp.arange(0, rot_dim, 2, dtype=F32) / rot_dim)
        ang = pos * inv_freq[None, :]
        cos, sin = jnp.cos(ang), jnp.sin(ang)
        j = lane % period
        idx = j % half
        in_lo = j < half
        in_hi = (j >= half) & (j < rot_dim)
        cos_l = jnp.take(cos, idx, axis=1)
        sin_l = jnp.take(sin, idx, axis=1)
        c = jnp.where((in_lo | in_hi)[None, :], cos_l, 1.0)
        s1 = jnp.where(in_lo[None, :], -sin_l, 0.0)
        s2 = jnp.where(in_hi[None, :], sin_l, 0.0)
        return jnp.stack([c, s1, s2]) * scale

    scale_a = HEAD_DIM ** -0.5 * LOG2E
    scale_b = DIFF_DIM ** -0.5 * LOG2E
    t = jnp.stack([
        tables(HEAD_DIM, HEAD_DIM // 4, scale_a),
        tables(HEAD_DIM, HEAD_DIM // 4, 1.0),
        tables(DIFF_DIM, DIFF_DIM // 4, scale_b),
        tables(DIFF_DIM, DIFF_DIM // 4, 1.0),
    ])
    return t[:, 0], t[:, 1], t[:, 2]


def _in_proj_kernel(x_ref, g_ref, w_ref, c_ref, s1_ref, s2_ref, oa_ref, ob_ref, h_ref, *, rows):
    n = pl.program_id(1)
    n_chunks = x_ref.shape[0] // rows

    def chunk(c):
        return pl.ds(c * rows, rows)

    def norm(c):
        h_ref[chunk(c), :] = (_rms_scale(x_ref[chunk(c), :], RMS_EPS) * g_ref[...]).astype(BF16)

    def matmul(c):
        return jnp.dot(h_ref[chunk(c), :], w_ref[...], preferred_element_type=F32)

    def rope_store(out_ref, shift):
        def store(c, acc):
            cos, s1, s2 = c_ref[0, chunk(c), :], s1_ref[0, chunk(c), :], s2_ref[0, chunk(c), :]
            for h in range(N_HEADS):
                cols = slice(h * HEAD_DIM, (h + 1) * HEAD_DIM)
                a = acc[:, cols]
                y = a * cos + pltpu.roll(a, HEAD_DIM - shift, 1) * s1 + pltpu.roll(a, shift, 1) * s2
                out_ref[chunk(c), cols] = y.astype(out_ref.dtype)
        return store

    def plain_store(out_ref):
        def store(c, acc):
            out_ref[chunk(c), :] = acc.astype(out_ref.dtype)
        return store

    def run(store, with_norm=False):
        accs = {}
        for i in range(n_chunks + 2):
            if with_norm and i < n_chunks:
                norm(i)
            if 1 <= i <= n_chunks:
                accs[i - 1] = matmul(i - 1)
            if i >= 2:
                store(i - 2, accs.pop(i - 2))

    pl.when(n == 0)(lambda: run(rope_store(oa_ref, HEAD_DIM // 8), with_norm=True))
    pl.when(n == 1)(lambda: run(rope_store(oa_ref, HEAD_DIM // 8)))
    pl.when(n == 2)(lambda: run(plain_store(oa_ref)))
    pl.when((n == 3) | (n == 4))(lambda: run(rope_store(ob_ref, DIFF_DIM // 8)))
    pl.when(n == 5)(lambda: run(plain_store(ob_ref)))


def _in_proj(x2d, gain, w_bf16, tabs, seq, tm=1024, rows=256):
    t = x2d.shape[0]
    c_t, s1_t, s2_t = tabs
    seq_tiles = seq // tm
    tab_idx = lambda m, n: (jnp.where(n < 2, n, jnp.clip(n - 1, 2, 3)), m % seq_tiles, 0)
    tab_spec = pl.BlockSpec((1, tm, LANES), tab_idx)
    return pl.pallas_call(
        functools.partial(_in_proj_kernel, rows=rows),
        grid=(t // tm, 6),
        in_specs=[
            pl.BlockSpec((tm, D_MODEL), lambda m, n: (m, 0)),
            pl.BlockSpec((1, D_MODEL), lambda m, n: (0, 0)),
            pl.BlockSpec((D_MODEL, GROUP_WIDTH), lambda m, n: (0, n)),
            tab_spec, tab_spec, tab_spec,
        ],
        out_specs=[
            pl.BlockSpec((tm, GROUP_WIDTH), lambda m, n: (m, jnp.minimum(n, 2))),
            pl.BlockSpec((tm, GROUP_WIDTH), lambda m, n: (m, jnp.maximum(n - 3, 0))),
        ],
        out_shape=[
            jax.ShapeDtypeStruct((t, 3 * GROUP_WIDTH), F32),
            jax.ShapeDtypeStruct((t, 3 * GROUP_WIDTH), BF16),
        ],
        scratch_shapes=[pltpu.VMEM((tm, D_MODEL), BF16)],
        compiler_params=_params(("arbitrary", "arbitrary")),
        name="in_proj",
    )(x2d, gain, w_bf16, c_t, s1_t, s2_t)


def _window_bias(n_q, n_k, offset):
    row = lax.broadcasted_iota(jnp.int32, (n_q, n_k), 0)
    col = lax.broadcasted_iota(jnp.int32, (n_q, n_k), 1)
    dist = col + offset - row
    return jnp.where(jnp.abs(dist) <= HALF_WINDOW, 0.0, NEG_BIAS).astype(F32)


def _attn_a_kernel(q_ref, k_ref, v_ref, o_ref, st_ref, qs_ref, ks_ref, vs_ref, ob_ref, lb_ref, *, seq):
    tq = 128
    srcs = (q_ref, k_ref, v_ref)
    dsts = (qs_ref, ks_ref, vs_ref)
    n_dil = len(DILATIONS)
    assert DILATIONS == (1, 4, 16)
    len4, len16 = seq // 4, seq // 16

    for a in range(3):
        dsts[a][0, :, :HEAD_DIM] = srcs[a][...].astype(BF16)
        for r4 in range(4):
            part = srcs[a][pl.ds(r4, len4, stride=4), :]
            st_ref[a, pl.ds(r4 * len4, len4), :] = part
            dsts[a][1, pl.ds(r4 * len4, len4), :HEAD_DIM] = part.astype(BF16)
        for r4 in range(4):
            for j in range(4):
                part = st_ref[a, pl.ds(r4 * len4 + j, len16, stride=4), :]
                dsts[a][2, pl.ds((r4 + 4 * j) * len16, len16), :HEAD_DIM] = part.astype(BF16)
    vs_ref[:, :, HEAD_DIM:] = jnp.ones((n_dil, seq, HEAD_DIM), BF16)

    tiles = []
    for g, dil in enumerate(DILATIONS):
        length = seq // dil
        kw = min(256, length)
        for r in range(dil):
            for tile in range(length // tq):
                q0 = tile * tq
                ws = min(max(q0 - HALF_WINDOW, 0), length - kw)
                tiles.append((g, dil, r, r * length, q0, ws, kw))

    def scores(t):
        g, dil, r, base, q0, ws, kw = t
        qt = qs_ref[g, pl.ds(base + q0, tq), :HEAD_DIM]
        kt = ks_ref[g, pl.ds(base + ws, kw), :HEAD_DIM]
        s = lax.dot_general(qt, kt, (((1,), (1,)), ((), ())), preferred_element_type=F32)
        return s + _window_bias(tq, kw, ws - q0)

    def softmax(s):
        m = jnp.max(s, axis=-1, keepdims=True)
        return jnp.exp2(s - m).astype(BF16), m

    def finish(t, p, m):
        g, dil, r, base, q0, ws, kw = t
        ov = jnp.dot(p, vs_ref[g, pl.ds(base + ws, kw), :], preferred_element_type=F32)
        den = ov[:, HEAD_DIM:]
        out_rows = pl.ds(q0 * dil + r, tq, stride=dil) if dil > 1 else pl.ds(q0, tq)
        ob_ref[g, out_rows, :] = ov[:, :HEAD_DIM] / den
        lb_ref[g, out_rows, :] = m + jnp.log2(den)

    s_vals, p_vals = {}, {}
    for i in range(len(tiles) + 2):
        if i < len(tiles):
            s_vals[i] = scores(tiles[i])
        if 1 <= i <= len(tiles):
            p_vals[i - 1] = softmax(s_vals.pop(i - 1))
        if i >= 2:
            finish(tiles[i - 2], *p_vals.pop(i - 2))

    l0, l1, l2 = lb_ref[0], lb_ref[1], lb_ref[2]
    mx = jnp.maximum(jnp.maximum(l0, l1), l2)
    e0, e1, e2 = jnp.exp2(l0 - mx), jnp.exp2(l1 - mx), jnp.exp2(l2 - mx)
    merged = (e0 * ob_ref[0] + e1 * ob_ref[1] + e2 * ob_ref[2]) / (e0 + e1 + e2)
    o_ref[...] = merged.astype(BF16)


def _attn_a(proj_a, batch, seq):
    t = batch * seq
    spec = lambda col0: pl.BlockSpec((seq, HEAD_DIM), lambda b, h: (b, col0 + h))
    return pl.pallas_call(
        functools.partial(_attn_a_kernel, seq=seq),
        grid=(batch, N_HEADS),
        in_specs=[spec(0), spec(N_HEADS), spec(2 * N_HEADS)],
        out_specs=pl.BlockSpec((seq, HEAD_DIM), lambda b, h: (b, h)),
        out_shape=jax.ShapeDtypeStruct((t, GROUP_WIDTH), BF16),
        scratch_shapes=[
            pltpu.VMEM((3, seq, HEAD_DIM), F32),
            pltpu.VMEM((len(DILATIONS), seq, HEAD_DIM), BF16),
            pltpu.VMEM((len(DILATIONS), seq, HEAD_DIM), BF16),
            pltpu.VMEM((len(DILATIONS), seq, 2 * HEAD_DIM), BF16),
            pltpu.VMEM((len(DILATIONS), seq, HEAD_DIM), F32),
            pltpu.VMEM((len(DILATIONS), seq, LANES), F32),
        ],
        compiler_params=_params(("arbitrary", "arbitrary")),
        name="attn_a",
    )(proj_a, proj_a, proj_a)


ONES_ROWS = 16
CAST_ROWS = 16


def _attn_b_kernel(*refs, tq, n_weights):
    q_ref, k_ref, v_ref, lq_ref, g_ref = refs[:5]
    w_f32_refs = refs[5:5 + n_weights]
    o_ref = refs[5 + n_weights]
    w_bf16_refs = refs[6 + n_weights:6 + 2 * n_weights]
    vt_ref = refs[6 + 2 * n_weights]

    seq = q_ref.shape[0]
    vt_ref[:HEAD_DIM, :] = v_ref[...].astype(F32).T.astype(BF16)
    vt_ref[HEAD_DIM:, :] = jnp.ones((ONES_ROWS, seq), BF16)

    cast_jobs = [(src, dst, r0) for src, dst in zip(w_f32_refs, w_bf16_refs)
                 for r0 in range(0, src.shape[0], CAST_ROWS)]

    lq = lq_ref[...]
    lam = (jnp.exp(jnp.sum(lq[0:1] * lq[1:2], axis=-1, keepdims=True))
           - jnp.exp(jnp.sum(lq[2:3] * lq[3:4], axis=-1, keepdims=True)) + LAM_INIT)

    k = k_ref[...]
    lane = lax.broadcasted_iota(jnp.int32, (tq, HEAD_DIM), 1)
    zero = jnp.zeros((tq, HEAD_DIM), BF16)

    def scores_t(qm):
        return lax.dot_general(k, qm, (((1,), (1,)), ((), ())), preferred_element_type=F32)

    def probs_t(st):
        return jnp.exp2(st - jnp.max(st, axis=0, keepdims=True)).astype(BF16)

    def weighted_v(pt):
        ov = jnp.dot(vt_ref[...], pt, preferred_element_type=F32)
        return ov[:HEAD_DIM] / ov[HEAD_DIM:HEAD_DIM + 1]

    n_tiles = seq // tq
    n_iters = n_tiles + 2
    scores, probs = {}, {}
    for i in range(n_iters):
        for src, dst, r0 in cast_jobs[i::n_iters]:
            dst[pl.ds(r0, CAST_ROWS), :] = src[pl.ds(r0, CAST_ROWS), :].astype(BF16)
        if i < n_tiles:
            q = q_ref[i * tq:(i + 1) * tq, :]
            scores[i] = (scores_t(jnp.where(lane < DIFF_DIM, q, zero)),
                         scores_t(jnp.where(lane >= DIFF_DIM, q, zero)))
        if 1 <= i <= n_tiles:
            st1, st2 = scores.pop(i - 1)
            probs[i - 1] = (probs_t(st1), probs_t(st2))
        if i >= 2:
            pt1, pt2 = probs.pop(i - 2)
            yt = weighted_v(pt1) - lam * weighted_v(pt2)
            yt = yt * lax.rsqrt(jnp.mean(yt * yt, axis=0, keepdims=True) + SUBLN_EPS)
            yt = yt * g_ref[...] * (1.0 - LAM_INIT)
            o_ref[(i - 2) * tq:(i - 1) * tq, :] = yt.T.astype(BF16)


def _attn_b(proj_b, lambda_qk, subln, weights, batch, seq, tq=256):
    t = batch * seq
    steps = batch * N_HEADS
    w_specs = []
    for w in weights:
        rows = w.shape[0] // steps
        assert rows * steps == w.shape[0] and rows % CAST_ROWS == 0
        w_specs.append(pl.BlockSpec((rows, w.shape[1]), lambda b, h: (b * N_HEADS + h, 0)))
    outs = pl.pallas_call(
        functools.partial(_attn_b_kernel, tq=tq, n_weights=len(weights)),
        grid=(batch, N_HEADS),
        in_specs=[
            pl.BlockSpec((seq, HEAD_DIM), lambda b, h: (b, h)),
            pl.BlockSpec((seq, HEAD_DIM), lambda b, h: (b, N_HEADS + h)),
            pl.BlockSpec((seq, HEAD_DIM), lambda b, h: (b, 2 * N_HEADS + h)),
            pl.BlockSpec((4, DIFF_DIM), lambda b, h: (0, 0)),
            pl.BlockSpec((HEAD_DIM, 1), lambda b, h: (0, 0)),
        ] + w_specs,
        out_specs=[pl.BlockSpec((seq, HEAD_DIM), lambda b, h: (b, h))] + w_specs,
        out_shape=[jax.ShapeDtypeStruct((t, GROUP_WIDTH), BF16)]
        + [jax.ShapeDtypeStruct(w.shape, BF16) for w in weights],
        scratch_shapes=[pltpu.VMEM((HEAD_DIM + ONES_ROWS, seq), BF16)],
        compiler_params=_params(("arbitrary", "arbitrary")),
        name="attn_b",
    )(proj_b, proj_b, proj_b, lambda_qk, subln, *weights)
    return outs[0], outs[1:]


def _out_proj_kernel(ya_ref, yb_ref, wa_ref, wb_ref, x_ref, o_ref):
    acc = jnp.dot(ya_ref[...], wa_ref[...], preferred_element_type=F32)
    acc = acc + jnp.dot(yb_ref[...], wb_ref[...], preferred_element_type=F32)
    o_ref[...] = x_ref[...] + acc


def _out_proj(ya, yb, w_bf16, x2d, tm=1024, tn=1024):
    t = x2d.shape[0]
    k_blocks = GROUP_WIDTH // GROUP_WIDTH
    return pl.pallas_call(
        _out_proj_kernel,
        grid=(t // tm, D_MODEL // tn),
        in_specs=[
            pl.BlockSpec((tm, GROUP_WIDTH), lambda m, n: (m, 0)),
            pl.BlockSpec((tm, GROUP_WIDTH), lambda m, n: (m, 0)),
            pl.BlockSpec((GROUP_WIDTH, tn), lambda m, n: (0, n)),
            pl.BlockSpec((GROUP_WIDTH, tn), lambda m, n: (k_blocks, n)),
            pl.BlockSpec((tm, tn), lambda m, n: (m, n)),
        ],
        out_specs=pl.BlockSpec((tm, tn), lambda m, n: (m, n)),
        out_shape=jax.ShapeDtypeStruct((t, D_MODEL), F32),
        compiler_params=_params(("arbitrary", "arbitrary")),
        name="out_proj",
    )(ya, yb, w_bf16, w_bf16, x2d)


def _ffn_kernel(x_ref, gn_ref, wg_ref, wu_ref, wd_ref, gf_ref, o_ref, h_ref, *, rows):
    j = pl.program_id(1)

    @pl.when(j == 0)
    def _():
        x = x_ref[...]
        h_ref[...] = (_rms_scale(x, RMS_EPS) * gn_ref[...]).astype(BF16)
        o_ref[...] = x

    n_chunks = x_ref.shape[0] // rows
    act = None
    for c in range(n_chunks + 1):
        if c < n_chunks:
            h = h_ref[c * rows:(c + 1) * rows, :]
            gate = jnp.dot(h, wg_ref[...], preferred_element_type=F32)
            up = jnp.dot(h, wu_ref[...], preferred_element_type=F32)
            new_act = (gate / (1.0 + jnp.exp(-gate)) * up).astype(BF16)
        if c >= 1:
            dst = pl.ds((c - 1) * rows, rows)
            o_ref[dst, :] += jnp.dot(act, wd_ref[...], preferred_element_type=F32)
        act = new_act

    @pl.when(j == pl.num_programs(1) - 1)
    def _():
        o_ref[...] = _rms_scale(o_ref[...], RMS_EPS) * gf_ref[...]


def _ffn(x2d, gain, wg_bf16, wu_bf16, wd_bf16, gain_final, tm=1024, tf=512, rows=256):
    t = x2d.shape[0]
    return pl.pallas_call(
        functools.partial(_ffn_kernel, rows=rows),
        grid=(t // tm, D_FF // tf),
        in_specs=[
            pl.BlockSpec((tm, D_MODEL), lambda m, j: (m, 0)),
            pl.BlockSpec((1, D_MODEL), lambda m, j: (0, 0)),
            pl.BlockSpec((D_MODEL, tf), lambda m, j: (0, j)),
            pl.BlockSpec((D_MODEL, tf), lambda m, j: (0, j)),
            pl.BlockSpec((tf, D_MODEL), lambda m, j: (j, 0)),
            pl.BlockSpec((1, D_MODEL), lambda m, j: (0, 0)),
        ],
        out_specs=pl.BlockSpec((tm, D_MODEL), lambda m, j: (m, 0)),
        out_shape=jax.ShapeDtypeStruct((t, D_MODEL), F32),
        scratch_shapes=[pltpu.VMEM((tm, D_MODEL), BF16)],
        compiler_params=_params(("arbitrary", "arbitrary")),
        name="ffn",
    )(x2d, gain, wg_bf16, wu_bf16, wd_bf16, gain_final)


def kernel(x, norm_attn, w_in, lambda_qk, subln, w_out, norm_ffn, w_gate, w_up, w_down, norm_final):
    batch, seq, d_model = x.shape
    assert d_model == D_MODEL and w_in.shape == (1, D_MODEL, 6 * GROUP_WIDTH)
    assert w_gate.shape == (1, D_MODEL, D_FF) and seq % 256 == 0
    x2d = x.reshape(batch * seq, D_MODEL)
    tabs = _rope_tables(seq)

    proj_a, proj_b = _in_proj(x2d, norm_attn[0][None, :], w_in[0].astype(BF16), tabs, seq)
    ya = _attn_a(proj_a, batch, seq)
    yb, (wo_b, wg_b, wu_b, wd_b) = _attn_b(proj_b, lambda_qk[0], subln[0][:, None],
                                           (w_out[0], w_gate[0], w_up[0], w_down[0]), batch, seq)
    x1 = _out_proj(ya, yb, wo_b, x2d)
    out = _ffn(x1, norm_ffn[0][None, :], wg_b, wu_b, wd_b, norm_final[None, :])
    return out.reshape(batch, seq, D_MODEL)
```

```python
import functools
import math

import jax
import jax.numpy as jnp
import numpy as np
from jax import lax
from jax.experimental import pallas as pl
from jax.experimental.pallas import tpu as pltpu

D_MODEL = 2048
HEAD_DIM = 128
N_HEADS = 8
GROUP_WIDTH = N_HEADS * HEAD_DIM
DIFF_DIM = HEAD_DIM // 2
D_FF = 5632
ROPE_THETA = 500000.0
RMS_EPS = 1e-6
SUBLN_EPS = 1e-5
NEG_BIAS = -1e30
HALF_WINDOW = 64
DILATIONS = (1, 4, 16)
LAM_INIT = 0.8 - 0.6 * math.exp(-0.3 * 0)
LOG2E = math.log2(math.e)

VMEM_LIMIT_BYTES = 56 * 1024 * 1024
LANES = 128

BF16 = jnp.bfloat16
F32 = jnp.float32


def _params(semantics):
    return pltpu.CompilerParams(dimension_semantics=semantics, vmem_limit_bytes=VMEM_LIMIT_BYTES)


def _rms_scale(x, eps):
    return x * lax.rsqrt(jnp.mean(x * x, axis=-1, keepdims=True) + eps)


def _rope_tables(seq):
    lane = np.arange(LANES)
    kinds = ((HEAD_DIM, HEAD_DIM // 4, HEAD_DIM ** -0.5 * LOG2E), (HEAD_DIM, HEAD_DIM // 4, 1.0),
             (DIFF_DIM, DIFF_DIM // 4, DIFF_DIM ** -0.5 * LOG2E), (DIFF_DIM, DIFF_DIM // 4, 1.0))
    freq, in_lo, in_hi = [], [], []
    for period, rot_dim, _ in kinds:
        half = rot_dim // 2
        j = lane % period
        inv_freq = ROPE_THETA ** (-jnp.arange(0, rot_dim, 2, dtype=F32) / rot_dim)
        freq.append(inv_freq[j % half])
        in_lo.append(j < half)
        in_hi.append((j >= half) & (j < rot_dim))
    in_lo, in_hi = np.stack(in_lo)[:, None, :], np.stack(in_hi)[:, None, :]
    scale = jnp.asarray([s for _, _, s in kinds], F32)[:, None, None]
    ang = jnp.arange(seq, dtype=F32)[None, :, None] * jnp.stack(freq)[:, None, :]
    cos, sin = jnp.cos(ang), jnp.sin(ang)
    c = jnp.where(in_lo | in_hi, cos, 1.0) * scale
    s1 = jnp.where(in_lo, -sin, 0.0) * scale
    s2 = jnp.where(in_hi, sin, 0.0) * scale
    return jnp.stack([c, s1, s2], axis=1)


def _in_proj_kernel(x_ref, g_ref, w_ref, t_ref, oa_ref, ob_ref, h_ref, *, rows):
    n = pl.program_id(1)
    n_chunks = x_ref.shape[0] // rows

    def chunk(c):
        return pl.ds(c * rows, rows)

    def norm(c):
        h_ref[chunk(c), :] = (_rms_scale(x_ref[chunk(c), :], RMS_EPS) * g_ref[...]).astype(BF16)

    def matmul(c):
        return jnp.dot(h_ref[chunk(c), :], w_ref[...], preferred_element_type=F32)

    def rope_store(out_ref, shift):
        def store(c, acc):
            cos, s1, s2 = (t_ref[0, i, chunk(c), :] for i in range(3))
            for h in range(N_HEADS):
                cols = slice(h * HEAD_DIM, (h + 1) * HEAD_DIM)
                a = acc[:, cols]
                y = a * cos + pltpu.roll(a, HEAD_DIM - shift, 1) * s1 + pltpu.roll(a, shift, 1) * s2
                out_ref[chunk(c), cols] = y.astype(out_ref.dtype)
        return store

    def plain_store(out_ref):
        def store(c, acc):
            out_ref[chunk(c), :] = acc.astype(out_ref.dtype)
        return store

    def run(store, with_norm=False):
        accs = {}
        for i in range(n_chunks + 2):
            if with_norm and i < n_chunks:
                norm(i)
            if 1 <= i <= n_chunks:
                accs[i - 1] = matmul(i - 1)
            if i >= 2:
                store(i - 2, accs.pop(i - 2))

    pl.when(n == 0)(lambda: run(rope_store(oa_ref, HEAD_DIM // 8), with_norm=True))
    pl.when(n == 1)(lambda: run(rope_store(oa_ref, HEAD_DIM // 8)))
    pl.when(n == 2)(lambda: run(plain_store(oa_ref)))
    pl.when((n == 3) | (n == 4))(lambda: run(rope_store(ob_ref, DIFF_DIM // 8)))
    pl.when(n == 5)(lambda: run(plain_store(ob_ref)))


def _in_proj(x2d, gain, w_bf16, tabs, seq, tm=1024, rows=256):
    t = x2d.shape[0]
    seq_tiles = seq // tm
    tab_idx = lambda m, n: (jnp.where(n < 2, n, jnp.clip(n - 1, 2, 3)), 0, m % seq_tiles, 0)
    tab_spec = pl.BlockSpec((1, 3, tm, LANES), tab_idx)
    return pl.pallas_call(
        functools.partial(_in_proj_kernel, rows=rows),
        grid=(t // tm, 6),
        in_specs=[
            pl.BlockSpec((tm, D_MODEL), lambda m, n: (m, 0)),
            pl.BlockSpec((1, D_MODEL), lambda m, n: (0, 0)),
            pl.BlockSpec((D_MODEL, GROUP_WIDTH), lambda m, n: (0, n)),
            tab_spec,
        ],
        out_specs=[
            pl.BlockSpec((tm, GROUP_WIDTH), lambda m, n: (m, jnp.minimum(n, 2))),
            pl.BlockSpec((tm, GROUP_WIDTH), lambda m, n: (m, jnp.maximum(n - 3, 0))),
        ],
        out_shape=[
            jax.ShapeDtypeStruct((t, 3 * GROUP_WIDTH), F32),
            jax.ShapeDtypeStruct((t, 3 * GROUP_WIDTH), BF16),
        ],
        scratch_shapes=[pltpu.VMEM((tm, D_MODEL), BF16)],
        compiler_params=_params(("arbitrary", "arbitrary")),
        name="in_proj",
    )(x2d, gain, w_bf16, tabs)


def _window_bias(n_q, n_k, offset):
    row = lax.broadcasted_iota(jnp.int32, (n_q, n_k), 0)
    col = lax.broadcasted_iota(jnp.int32, (n_q, n_k), 1)
    dist = col + offset - row
    return jnp.where(jnp.abs(dist) <= HALF_WINDOW, 0.0, NEG_BIAS).astype(F32)


def _attn_a_kernel(*refs, seq, heads):
    in_refs = refs[:3 * heads]
    o_ref = refs[3 * heads]
    st_ref, qs_ref, ks_ref, vs_ref, ob_ref, lb_ref = refs[3 * heads + 1:]
    tq = 128
    n_merge = 1 if heads == 1 else seq // tq
    n_dil = len(DILATIONS)
    assert DILATIONS == (1, 4, 16)
    len4, len16 = seq // 4, seq // 16

    def relayout(hd):
        for a, dst in enumerate((qs_ref, ks_ref, vs_ref)):
            src = in_refs[3 * hd + a]
            dst[hd, 0, :, :HEAD_DIM] = src[...].astype(BF16)
            for r4 in range(4):
                part = src[pl.ds(r4, len4, stride=4), :]
                st_ref[a, pl.ds(r4 * len4, len4), :] = part
                dst[hd, 1, pl.ds(r4 * len4, len4), :HEAD_DIM] = part.astype(BF16)
            for r4 in range(4):
                for j in range(4):
                    part = st_ref[a, pl.ds(r4 * len4 + j, len16, stride=4), :]
                    dst[hd, 2, pl.ds((r4 + 4 * j) * len16, len16), :HEAD_DIM] = part.astype(BF16)
        vs_ref[hd, :, :, HEAD_DIM:] = jnp.ones((n_dil, seq, HEAD_DIM), BF16)

    tiles = []
    for g, dil in enumerate(DILATIONS):
        length = seq // dil
        kw = min(256, length)
        for r in range(dil):
            for tile in range(length // tq):
                q0 = tile * tq
                ws = min(max(q0 - HALF_WINDOW, 0), length - kw)
                tiles.append((g, dil, r, r * length, q0, ws, kw))

    def scores(hd, t):
        g, dil, r, base, q0, ws, kw = t
        qt = qs_ref[hd, g, pl.ds(base + q0, tq), :]
        kt = ks_ref[hd, g, pl.ds(base + ws, kw), :]
        s = lax.dot_general(qt, kt, (((1,), (1,)), ((), ())), preferred_element_type=F32)
        return s + _window_bias(tq, kw, ws - q0)

    def softmax(s):
        m = jnp.max(s, axis=-1, keepdims=True)
        return jnp.exp2(s - m).astype(BF16), m

    def finish(hd, t, p, m):
        g, dil, r, base, q0, ws, kw = t
        ov = jnp.dot(p, vs_ref[hd, g, pl.ds(base + ws, kw), :], preferred_element_type=F32)
        den = ov[:, HEAD_DIM:]
        out_rows = pl.ds(q0 * dil + r, tq, stride=dil) if dil > 1 else pl.ds(q0, tq)
        ob_ref[hd, g, out_rows, :] = ov[:, :HEAD_DIM] / den
        lb_ref[hd, g, out_rows, :] = m + jnp.log2(den)

    merge_rows = seq // n_merge

    def merge(hd, c):
        rows = pl.ds(c * merge_rows, merge_rows)
        l0, l1, l2 = lb_ref[hd, 0, rows, :], lb_ref[hd, 1, rows, :], lb_ref[hd, 2, rows, :]
        mx = jnp.maximum(jnp.maximum(l0, l1), l2)
        e0, e1, e2 = jnp.exp2(l0 - mx), jnp.exp2(l1 - mx), jnp.exp2(l2 - mx)
        merged = (e0 * ob_ref[hd, 0, rows, :] + e1 * ob_ref[hd, 1, rows, :]
                  + e2 * ob_ref[hd, 2, rows, :]) / (e0 + e1 + e2)
        o_ref[rows, hd * HEAD_DIM:(hd + 1) * HEAD_DIM] = merged.astype(BF16)

    for hd in range(heads):
        relayout(hd)
        s_vals, p_vals = {}, {}
        for i in range(len(tiles) + 2):
            if i < len(tiles):
                s_vals[i] = scores(hd, tiles[i])
            if 1 <= i <= len(tiles):
                p_vals[i - 1] = softmax(s_vals.pop(i - 1))
            if i >= 2:
                finish(hd, tiles[i - 2], *p_vals.pop(i - 2))
            if hd > 0 and i < n_merge:
                merge(hd - 1, i)
    for c in range(n_merge):
        merge(heads - 1, c)


def _attn_a(proj_a, batch, seq, heads=1):
    t = batch * seq
    in_specs = []
    for hd in range(heads):
        for a in range(3):
            in_specs.append(pl.BlockSpec(
                (seq, HEAD_DIM), lambda b, h, hd=hd, a=a: (b, a * N_HEADS + h * heads + hd)))
    n_dil = len(DILATIONS)
    return pl.pallas_call(
        functools.partial(_attn_a_kernel, seq=seq, heads=heads),
        grid=(batch, N_HEADS // heads),
        in_specs=in_specs,
        out_specs=pl.BlockSpec((seq, heads * HEAD_DIM), lambda b, h: (b, h)),
        out_shape=jax.ShapeDtypeStruct((t, GROUP_WIDTH), BF16),
        scratch_shapes=[
            pltpu.VMEM((3, seq, HEAD_DIM), F32),
            pltpu.VMEM((heads, n_dil, seq, HEAD_DIM), BF16),
            pltpu.VMEM((heads, n_dil, seq, HEAD_DIM), BF16),
            pltpu.VMEM((heads, n_dil, seq, 2 * HEAD_DIM), BF16),
            pltpu.VMEM((heads, n_dil, seq, HEAD_DIM), F32),
            pltpu.VMEM((heads, n_dil, seq, LANES), F32),
        ],
        compiler_params=_params(("arbitrary", "arbitrary")),
        name="attn_a",
    )(*([proj_a] * (3 * heads)))


ONES_ROWS = 16
CAST_ROWS = 16


def _attn_b_kernel(*refs, tq, n_weights, heads):
    q_ref, k_ref, v_ref, lq_ref, g_ref = refs[:5]
    w_f32_refs = refs[5:5 + n_weights]
    o_ref = refs[5 + n_weights]
    w_bf16_refs = refs[6 + n_weights:6 + 2 * n_weights]
    vt_ref = refs[6 + 2 * n_weights]

    seq = q_ref.shape[0]
    for hd in range(heads):
        v = v_ref[:, hd * HEAD_DIM:(hd + 1) * HEAD_DIM]
        vt_ref[hd, :HEAD_DIM, :] = v.astype(F32).T.astype(BF16)
        vt_ref[hd, HEAD_DIM:, :] = jnp.ones((ONES_ROWS, seq), BF16)

    cast_jobs = [(src, dst, r0) for src, dst in zip(w_f32_refs, w_bf16_refs)
                 for r0 in range(0, src.shape[0], CAST_ROWS)]

    lq = lq_ref[...]
    lam = (jnp.exp(jnp.sum(lq[0:1] * lq[1:2], axis=-1, keepdims=True))
           - jnp.exp(jnp.sum(lq[2:3] * lq[3:4], axis=-1, keepdims=True)) + LAM_INIT)

    lane = lax.broadcasted_iota(jnp.int32, (tq, HEAD_DIM), 1)
    zero = jnp.zeros((tq, HEAD_DIM), BF16)

    def scores_t(hd, qm):
        k = k_ref[:, hd * HEAD_DIM:(hd + 1) * HEAD_DIM]
        return lax.dot_general(k, qm, (((1,), (1,)), ((), ())), preferred_element_type=F32)

    def probs_t(st):
        return jnp.exp2(st - jnp.max(st, axis=0, keepdims=True)).astype(BF16)

    def weighted_v(hd, pt):
        ov = jnp.dot(vt_ref[hd], pt, preferred_element_type=F32)
        return ov[:HEAD_DIM] / ov[HEAD_DIM:HEAD_DIM + 1]

    tiles = [(hd, i * tq) for hd in range(heads) for i in range(seq // tq)]
    n_iters = len(tiles) + 2
    scores, probs = {}, {}
    for i in range(n_iters):
        for src, dst, r0 in cast_jobs[i::n_iters]:
            dst[pl.ds(r0, CAST_ROWS), :] = src[pl.ds(r0, CAST_ROWS), :].astype(BF16)
        if i < len(tiles):
            hd, q0 = tiles[i]
            q = q_ref[q0:q0 + tq, hd * HEAD_DIM:(hd + 1) * HEAD_DIM]
            scores[i] = (scores_t(hd, jnp.where(lane < DIFF_DIM, q, zero)),
                         scores_t(hd, jnp.where(lane >= DIFF_DIM, q, zero)))
        if 1 <= i <= len(tiles):
            st1, st2 = scores.pop(i - 1)
            probs[i - 1] = (probs_t(st1), probs_t(st2))
        if i >= 2:
            hd, q0 = tiles[i - 2]
            pt1, pt2 = probs.pop(i - 2)
            yt = weighted_v(hd, pt1) - lam * weighted_v(hd, pt2)
            yt = yt * lax.rsqrt(jnp.mean(yt * yt, axis=0, keepdims=True) + SUBLN_EPS)
            yt = yt * g_ref[...] * (1.0 - LAM_INIT)
            o_ref[q0:q0 + tq, hd * HEAD_DIM:(hd + 1) * HEAD_DIM] = yt.T.astype(BF16)


def _attn_b(proj_b, lambda_qk, subln, weights, batch, seq, tq=256, heads=1):
    t = batch * seq
    groups = N_HEADS // heads
    steps = batch * groups
    width = heads * HEAD_DIM
    w_specs = []
    for w in weights:
        rows = w.shape[0] // steps
        assert rows * steps == w.shape[0] and rows % CAST_ROWS == 0
        w_specs.append(pl.BlockSpec((rows, w.shape[1]), lambda b, h: (b * groups + h, 0)))
    outs = pl.pallas_call(
        functools.partial(_attn_b_kernel, tq=tq, n_weights=len(weights), heads=heads),
        grid=(batch, groups),
        in_specs=[
            pl.BlockSpec((seq, width), lambda b, h: (b, h)),
            pl.BlockSpec((seq, width), lambda b, h: (b, groups + h)),
            pl.BlockSpec((seq, width), lambda b, h: (b, 2 * groups + h)),
            pl.BlockSpec((4, DIFF_DIM), lambda b, h: (0, 0)),
            pl.BlockSpec((HEAD_DIM, 1), lambda b, h: (0, 0)),
        ] + w_specs,
        out_specs=[pl.BlockSpec((seq, width), lambda b, h: (b, h))] + w_specs,
        out_shape=[jax.ShapeDtypeStruct((t, GROUP_WIDTH), BF16)]
        + [jax.ShapeDtypeStruct(w.shape, BF16) for w in weights],
        scratch_shapes=[pltpu.VMEM((heads, HEAD_DIM + ONES_ROWS, seq), BF16)],
        compiler_params=_params(("arbitrary", "arbitrary")),
        name="attn_b",
    )(proj_b, proj_b, proj_b, lambda_qk, subln, *weights)
    return outs[0], outs[1:]


def _out_proj_kernel(ya_ref, yb_ref, wa_ref, wb_ref, x_ref, o_ref):
    acc = jnp.dot(ya_ref[...], wa_ref[...], preferred_element_type=F32)
    acc = acc + jnp.dot(yb_ref[...], wb_ref[...], preferred_element_type=F32)
    o_ref[...] = x_ref[...] + acc


def _out_proj(ya, yb, w_bf16, x2d, tm=1024, tn=1024):
    t = x2d.shape[0]
    k_blocks = GROUP_WIDTH // GROUP_WIDTH
    return pl.pallas_call(
        _out_proj_kernel,
        grid=(t // tm, D_MODEL // tn),
        in_specs=[
            pl.BlockSpec((tm, GROUP_WIDTH), lambda m, n: (m, 0)),
            pl.BlockSpec((tm, GROUP_WIDTH), lambda m, n: (m, 0)),
            pl.BlockSpec((GROUP_WIDTH, tn), lambda m, n: (0, n)),
            pl.BlockSpec((GROUP_WIDTH, tn), lambda m, n: (k_blocks, n)),
            pl.BlockSpec((tm, tn), lambda m, n: (m, n)),
        ],
        out_specs=pl.BlockSpec((tm, tn), lambda m, n: (m, n)),
        out_shape=jax.ShapeDtypeStruct((t, D_MODEL), F32),
        compiler_params=_params(("arbitrary", "arbitrary")),
        name="out_proj",
    )(ya, yb, w_bf16, w_bf16, x2d)


def _ffn_kernel(x_ref, gn_ref, wg_ref, wu_ref, wd_ref, gf_ref, o_ref, h_ref, *, rows):
    j = pl.program_id(1)
    last = pl.num_programs(1) - 1
    n_chunks = x_ref.shape[0] // rows

    def chunk(c):
        return pl.ds(c * rows, rows)

    def norm(c):
        h_ref[chunk(c), :] = (_rms_scale(x_ref[chunk(c), :], RMS_EPS) * gn_ref[...]).astype(BF16)

    def run(first, final):
        acts = {}
        for i in range(n_chunks + 2):
            if first and i < n_chunks:
                norm(i)
            if 1 <= i <= n_chunks:
                h = h_ref[chunk(i - 1), :]
                gate = jnp.dot(h, wg_ref[...], preferred_element_type=F32)
                up = jnp.dot(h, wu_ref[...], preferred_element_type=F32)
                acts[i - 1] = (gate / (1.0 + jnp.exp(-gate)) * up).astype(BF16)
            if i >= 2:
                dst = chunk(i - 2)
                base = x_ref[dst, :] if first else o_ref[dst, :]
                acc = base + jnp.dot(acts.pop(i - 2), wd_ref[...], preferred_element_type=F32)
                o_ref[dst, :] = _rms_scale(acc, RMS_EPS) * gf_ref[...] if final else acc

    pl.when(j == 0)(lambda: run(True, False))
    pl.when((j > 0) & (j < last))(lambda: run(False, False))
    pl.when(j == last)(lambda: run(False, True))


def _ffn(x2d, gain, wg_bf16, wu_bf16, wd_bf16, gain_final, tm=1024, tf=512, rows=256):
    t = x2d.shape[0]
    assert D_FF // tf >= 2
    return pl.pallas_call(
        functools.partial(_ffn_kernel, rows=rows),
        grid=(t // tm, D_FF // tf),
        in_specs=[
            pl.BlockSpec((tm, D_MODEL), lambda m, j: (m, 0)),
            pl.BlockSpec((1, D_MODEL), lambda m, j: (0, 0)),
            pl.BlockSpec((D_MODEL, tf), lambda m, j: (0, j)),
            pl.BlockSpec((D_MODEL, tf), lambda m, j: (0, j)),
            pl.BlockSpec((tf, D_MODEL), lambda m, j: (j, 0)),
            pl.BlockSpec((1, D_MODEL), lambda m, j: (0, 0)),
        ],
        out_specs=pl.BlockSpec((tm, D_MODEL), lambda m, j: (m, 0)),
        out_shape=jax.ShapeDtypeStruct((t, D_MODEL), F32),
        scratch_shapes=[pltpu.VMEM((tm, D_MODEL), BF16)],
        compiler_params=_params(("arbitrary", "arbitrary")),
        name="ffn",
    )(x2d, gain, wg_bf16, wu_bf16, wd_bf16, gain_final)


def kernel(x, norm_attn, w_in, lambda_qk, subln, w_out, norm_ffn, w_gate, w_up, w_down, norm_final):
    batch, seq, d_model = x.shape
    assert d_model == D_MODEL and w_in.shape == (1, D_MODEL, 6 * GROUP_WIDTH)
    assert w_gate.shape == (1, D_MODEL, D_FF) and seq % 256 == 0
    x2d = x.reshape(batch * seq, D_MODEL)
    tabs = _rope_tables(seq)

    proj_a, proj_b = _in_proj(x2d, norm_attn[0][None, :], w_in[0].astype(BF16), tabs, seq)
    ya = _attn_a(proj_a, batch, seq)
    yb, (wo_b, wg_b, wu_b, wd_b) = _attn_b(proj_b, lambda_qk[0], subln[0][:, None],
                                           (w_out[0], w_gate[0], w_up[0], w_down[0]), batch, seq)
    x1 = _out_proj(ya, yb, wo_b, x2d)
    out = _ffn(x1, norm_ffn[0][None, :], wg_b, wu_b, wd_b, norm_final[None, :])
    return out.reshape(batch, seq, D_MODEL)
```

```python
import functools
import math

import jax
import jax.numpy as jnp
import numpy as np
from jax import lax
from jax.experimental import pallas as pl
from jax.experimental.pallas import tpu as pltpu

D_MODEL = 2048
HEAD_DIM = 128
N_HEADS = 8
GROUP_WIDTH = N_HEADS * HEAD_DIM
DIFF_DIM = HEAD_DIM // 2
D_FF = 5632
ROPE_THETA = 500000.0
RMS_EPS = 1e-6
SUBLN_EPS = 1e-5
NEG_BIAS = -1e30
HALF_WINDOW = 64
DILATIONS = (1, 4, 16)
LAM_INIT = 0.8 - 0.6 * math.exp(-0.3 * 0)
LOG2E = math.log2(math.e)

VMEM_LIMIT_BYTES = 56 * 1024 * 1024
LANES = 128

BF16 = jnp.bfloat16
F32 = jnp.float32


def _params(semantics):
    return pltpu.CompilerParams(dimension_semantics=semantics, vmem_limit_bytes=VMEM_LIMIT_BYTES)


def _rms_scale(x, eps):
    return x * lax.rsqrt(jnp.mean(x * x, axis=-1, keepdims=True) + eps)


def _rope_tables(seq):
    lane = np.arange(LANES)
    kinds = ((HEAD_DIM, HEAD_DIM // 4, HEAD_DIM ** -0.5 * LOG2E), (HEAD_DIM, HEAD_DIM // 4, 1.0),
             (DIFF_DIM, DIFF_DIM // 4, DIFF_DIM ** -0.5 * LOG2E), (DIFF_DIM, DIFF_DIM // 4, 1.0))
    pos = jnp.arange(seq, dtype=F32)[:, None]
    cos, sin, in_lo, in_hi = [], [], [], []
    for period, rot_dim, _ in kinds:
        half = rot_dim // 2
        j = lane % period
        inv_freq = ROPE_THETA ** (-jnp.arange(0, rot_dim, 2, dtype=F32) / rot_dim)
        ang = pos * inv_freq[None, :]
        cos.append(jnp.take(jnp.cos(ang), j % half, axis=1))
        sin.append(jnp.take(jnp.sin(ang), j % half, axis=1))
        in_lo.append(j < half)
        in_hi.append((j >= half) & (j < rot_dim))
    in_lo, in_hi = np.stack(in_lo)[:, None, :], np.stack(in_hi)[:, None, :]
    scale = jnp.asarray([s for _, _, s in kinds], F32)[:, None, None]
    cos, sin = jnp.stack(cos), jnp.stack(sin)
    c = jnp.where(in_lo | in_hi, cos, 1.0) * scale
    s1 = jnp.where(in_lo, -sin, 0.0) * scale
    s2 = jnp.where(in_hi, sin, 0.0) * scale
    return jnp.stack([c, s1, s2], axis=1)


def _in_proj_kernel(x_ref, g_ref, w_ref, t_ref, oa_ref, ob_ref, h_ref, *, rows):
    n = pl.program_id(1)
    n_chunks = x_ref.shape[0] // rows

    def chunk(c):
        return pl.ds(c * rows, rows)

    def norm(c):
        h_ref[chunk(c), :] = (_rms_scale(x_ref[chunk(c), :], RMS_EPS) * g_ref[...]).astype(BF16)

    def matmul(c):
        return jnp.dot(h_ref[chunk(c), :], w_ref[...], preferred_element_type=F32)

    def rope_store(out_ref, shift):
        def store(c, acc):
            cos, s1, s2 = (t_ref[0, i, chunk(c), :] for i in range(3))
            for h in range(N_HEADS):
                cols = slice(h * HEAD_DIM, (h + 1) * HEAD_DIM)
                a = acc[:, cols]
                y = a * cos + pltpu.roll(a, HEAD_DIM - shift, 1) * s1 + pltpu.roll(a, shift, 1) * s2
                out_ref[chunk(c), cols] = y.astype(out_ref.dtype)
        return store

    def plain_store(out_ref):
        def store(c, acc):
            out_ref[chunk(c), :] = acc.astype(out_ref.dtype)
        return store

    def run(store, with_norm=False):
        accs = {}
        for i in range(n_chunks + 2):
            if with_norm and i < n_chunks:
                norm(i)
            if 1 <= i <= n_chunks:
                accs[i - 1] = matmul(i - 1)
            if i >= 2:
                store(i - 2, accs.pop(i - 2))

    pl.when(n == 0)(lambda: run(rope_store(oa_ref, HEAD_DIM // 8), with_norm=True))
    pl.when(n == 1)(lambda: run(rope_store(oa_ref, HEAD_DIM // 8)))
    pl.when(n == 2)(lambda: run(plain_store(oa_ref)))
    pl.when((n == 3) | (n == 4))(lambda: run(rope_store(ob_ref, DIFF_DIM // 8)))
    pl.when(n == 5)(lambda: run(plain_store(ob_ref)))


def _in_proj(x2d, gain, w_bf16, tabs, seq, tm=1024, rows=256):
    t = x2d.shape[0]
    seq_tiles = seq // tm
    tab_idx = lambda m, n: (jnp.where(n < 2, n, jnp.clip(n - 1, 2, 3)), 0, m % seq_tiles, 0)
    tab_spec = pl.BlockSpec((1, 3, tm, LANES), tab_idx)
    return pl.pallas_call(
        functools.partial(_in_proj_kernel, rows=rows),
        grid=(t // tm, 6),
        in_specs=[
            pl.BlockSpec((tm, D_MODEL), lambda m, n: (m, 0)),
            pl.BlockSpec((1, D_MODEL), lambda m, n: (0, 0)),
            pl.BlockSpec((D_MODEL, GROUP_WIDTH), lambda m, n: (0, n)),
            tab_spec,
        ],
        out_specs=[
            pl.BlockSpec((tm, GROUP_WIDTH), lambda m, n: (m, jnp.minimum(n, 2))),
            pl.BlockSpec((tm, GROUP_WIDTH), lambda m, n: (m, jnp.maximum(n - 3, 0))),
        ],
        out_shape=[
            jax.ShapeDtypeStruct((t, 3 * GROUP_WIDTH), F32),
            jax.ShapeDtypeStruct((t, 3 * GROUP_WIDTH), BF16),
        ],
        scratch_shapes=[pltpu.VMEM((tm, D_MODEL), BF16)],
        compiler_params=_params(("arbitrary", "arbitrary")),
        name="in_proj",
    )(x2d, gain, w_bf16, tabs)


def _window_bias(n_q, n_k, offset):
    row = lax.broadcasted_iota(jnp.int32, (n_q, n_k), 0)
    col = lax.broadcasted_iota(jnp.int32, (n_q, n_k), 1)
    dist = col + offset - row
    return jnp.where(jnp.abs(dist) <= HALF_WINDOW, 0.0, NEG_BIAS).astype(F32)


def _attn_a_kernel(*refs, seq, heads):
    in_refs = refs[:3 * heads]
    o_ref = refs[3 * heads]
    st_ref, qs_ref, ks_ref, vs_ref, ob_ref, lb_ref = refs[3 * heads + 1:]
    tq = 128
    n_merge = 1 if heads == 1 else seq // tq
    n_dil = len(DILATIONS)
    assert DILATIONS == (1, 4, 16)
    len4, len16 = seq // 4, seq // 16

    def relayout(hd):
        for a, dst in enumerate((qs_ref, ks_ref, vs_ref)):
            src = in_refs[3 * hd + a]
            dst[hd, 0, :, :HEAD_DIM] = src[...].astype(BF16)
            for r4 in range(4):
                part = src[pl.ds(r4, len4, stride=4), :]
                st_ref[a, pl.ds(r4 * len4, len4), :] = part
                dst[hd, 1, pl.ds(r4 * len4, len4), :HEAD_DIM] = part.astype(BF16)
            for r4 in range(4):
                for j in range(4):
                    part = st_ref[a, pl.ds(r4 * len4 + j, len16, stride=4), :]
                    dst[hd, 2, pl.ds((r4 + 4 * j) * len16, len16), :HEAD_DIM] = part.astype(BF16)
        vs_ref[hd, :, :, HEAD_DIM:] = jnp.ones((n_dil, seq, HEAD_DIM), BF16)

    tiles = []
    for g, dil in enumerate(DILATIONS):
        length = seq // dil
        kw = min(256, length)
        for r in range(dil):
            for tile in range(length // tq):
                q0 = tile * tq
                ws = min(max(q0 - HALF_WINDOW, 0), length - kw)
                tiles.append((g, dil, r, r * length, q0, ws, kw))

    def scores(hd, t):
        g, dil, r, base, q0, ws, kw = t
        qt = qs_ref[hd, g, pl.ds(base + q0, tq), :]
        kt = ks_ref[hd, g, pl.ds(base + ws, kw), :]
        s = lax.dot_general(qt, kt, (((1,), (1,)), ((), ())), preferred_element_type=F32)
        return s + _window_bias(tq, kw, ws - q0)

    def softmax(s):
        m = jnp.max(s, axis=-1, keepdims=True)
        return jnp.exp2(s - m).astype(BF16), m

    def finish(hd, t, p, m):
        g, dil, r, base, q0, ws, kw = t
        ov = jnp.dot(p, vs_ref[hd, g, pl.ds(base + ws, kw), :], preferred_element_type=F32)
        den = ov[:, HEAD_DIM:]
        out_rows = pl.ds(q0 * dil + r, tq, stride=dil) if dil > 1 else pl.ds(q0, tq)
        ob_ref[hd, g, out_rows, :] = ov[:, :HEAD_DIM] / den
        lb_ref[hd, g, out_rows, :] = m + jnp.log2(den)

    merge_rows = seq // n_merge

    def merge(hd, c):
        rows = pl.ds(c * merge_rows, merge_rows)
        l0, l1, l2 = lb_ref[hd, 0, rows, :], lb_ref[hd, 1, rows, :], lb_ref[hd, 2, rows, :]
        mx = jnp.maximum(jnp.maximum(l0, l1), l2)
        e0, e1, e2 = jnp.exp2(l0 - mx), jnp.exp2(l1 - mx), jnp.exp2(l2 - mx)
        merged = (e0 * ob_ref[hd, 0, rows, :] + e1 * ob_ref[hd, 1, rows, :]
                  + e2 * ob_ref[hd, 2, rows, :]) / (e0 + e1 + e2)
        o_ref[rows, hd * HEAD_DIM:(hd + 1) * HEAD_DIM] = merged.astype(BF16)

    for hd in range(heads):
        relayout(hd)
        s_vals, p_vals = {}, {}
        for i in range(len(tiles) + 2):
            if i < len(tiles):
                s_vals[i] = scores(hd, tiles[i])
            if 1 <= i <= len(tiles):
                p_vals[i - 1] = softmax(s_vals.pop(i - 1))
            if i >= 2:
                finish(hd, tiles[i - 2], *p_vals.pop(i - 2))
            if hd > 0 and i < n_merge:
                merge(hd - 1, i)
    for c in range(n_merge):
        merge(heads - 1, c)


def _attn_a(proj_a, batch, seq, heads=1):
    t = batch * seq
    in_specs = []
    for hd in range(heads):
        for a in range(3):
            in_specs.append(pl.BlockSpec(
                (seq, HEAD_DIM), lambda b, h, hd=hd, a=a: (b, a * N_HEADS + h * heads + hd)))
    n_dil = len(DILATIONS)
    return pl.pallas_call(
        functools.partial(_attn_a_kernel, seq=seq, heads=heads),
        grid=(batch, N_HEADS // heads),
        in_specs=in_specs,
        out_specs=pl.BlockSpec((seq, heads * HEAD_DIM), lambda b, h: (b, h)),
        out_shape=jax.ShapeDtypeStruct((t, GROUP_WIDTH), BF16),
        scratch_shapes=[
            pltpu.VMEM((3, seq, HEAD_DIM), F32),
            pltpu.VMEM((heads, n_dil, seq, HEAD_DIM), BF16),
            pltpu.VMEM((heads, n_dil, seq, HEAD_DIM), BF16),
            pltpu.VMEM((heads, n_dil, seq, 2 * HEAD_DIM), BF16),
            pltpu.VMEM((heads, n_dil, seq, HEAD_DIM), F32),
            pltpu.VMEM((heads, n_dil, seq, LANES), F32),
        ],
        compiler_params=_params(("arbitrary", "arbitrary")),
        name="attn_a",
    )(*([proj_a] * (3 * heads)))


ONES_ROWS = 16
CAST_ROWS = 16


def _attn_b_kernel(*refs, tq, n_weights, heads):
    q_ref, k_ref, v_ref, lq_ref, g_ref = refs[:5]
    w_f32_refs = refs[5:5 + n_weights]
    o_ref = refs[5 + n_weights]
    w_bf16_refs = refs[6 + n_weights:6 + 2 * n_weights]
    vt_ref = refs[6 + 2 * n_weights]

    seq = q_ref.shape[0]
    for hd in range(heads):
        v = v_ref[:, hd * HEAD_DIM:(hd + 1) * HEAD_DIM]
        vt_ref[hd, :HEAD_DIM, :] = v.astype(F32).T.astype(BF16)
        vt_ref[hd, HEAD_DIM:, :] = jnp.ones((ONES_ROWS, seq), BF16)

    cast_jobs = [(src, dst, r0) for src, dst in zip(w_f32_refs, w_bf16_refs)
                 for r0 in range(0, src.shape[0], CAST_ROWS)]

    lq = lq_ref[...]
    lam = (jnp.exp(jnp.sum(lq[0:1] * lq[1:2], axis=-1, keepdims=True))
           - jnp.exp(jnp.sum(lq[2:3] * lq[3:4], axis=-1, keepdims=True)) + LAM_INIT)

    lane = lax.broadcasted_iota(jnp.int32, (tq, HEAD_DIM), 1)
    zero = jnp.zeros((tq, HEAD_DIM), BF16)

    def scores_t(hd, qm):
        k = k_ref[:, hd * HEAD_DIM:(hd + 1) * HEAD_DIM]
        return lax.dot_general(k, qm, (((1,), (1,)), ((), ())), preferred_element_type=F32)

    def probs_t(st):
        return jnp.exp2(st - jnp.max(st, axis=0, keepdims=True)).astype(BF16)

    def weighted_v(hd, pt):
        ov = jnp.dot(vt_ref[hd], pt, preferred_element_type=F32)
        return ov[:HEAD_DIM] / ov[HEAD_DIM:HEAD_DIM + 1]

    tiles = [(hd, i * tq) for hd in range(heads) for i in range(seq // tq)]
    n_iters = len(tiles) + 2
    scores, probs = {}, {}
    for i in range(n_iters):
        for src, dst, r0 in cast_jobs[i::n_iters]:
            dst[pl.ds(r0, CAST_ROWS), :] = src[pl.ds(r0, CAST_ROWS), :].astype(BF16)
        if i < len(tiles):
            hd, q0 = tiles[i]
            q = q_ref[q0:q0 + tq, hd * HEAD_DIM:(hd + 1) * HEAD_DIM]
            scores[i] = (scores_t(hd, jnp.where(lane < DIFF_DIM, q, zero)),
                         scores_t(hd, jnp.where(lane >= DIFF_DIM, q, zero)))
        if 1 <= i <= len(tiles):
            st1, st2 = scores.pop(i - 1)
            probs[i - 1] = (probs_t(st1), probs_t(st2))
        if i >= 2:
            hd, q0 = tiles[i - 2]
            pt1, pt2 = probs.pop(i - 2)
            yt = weighted_v(hd, pt1) - lam * weighted_v(hd, pt2)
            yt = yt * lax.rsqrt(jnp.mean(yt * yt, axis=0, keepdims=True) + SUBLN_EPS)
            yt = yt * g_ref[...] * (1.0 - LAM_INIT)
            o_ref[q0:q0 + tq, hd * HEAD_DIM:(hd + 1) * HEAD_DIM] = yt.T.astype(BF16)


def _attn_b(proj_b, lambda_qk, subln, weights, batch, seq, tq=256, heads=1):
    t = batch * seq
    groups = N_HEADS // heads
    steps = batch * groups
    width = heads * HEAD_DIM
    w_specs = []
    for w in weights:
        rows = w.shape[0] // steps
        assert rows * steps == w.shape[0] and rows % CAST_ROWS == 0
        w_specs.append(pl.BlockSpec((rows, w.shape[1]), lambda b, h: (b * groups + h, 0)))
    outs = pl.pallas_call(
        functools.partial(_attn_b_kernel, tq=tq, n_weights=len(weights), heads=heads),
        grid=(batch, groups),
        in_specs=[
            pl.BlockSpec((seq, width), lambda b, h: (b, h)),
            pl.BlockSpec((seq, width), lambda b, h: (b, groups + h)),
            pl.BlockSpec((seq, width), lambda b, h: (b, 2 * groups + h)),
            pl.BlockSpec((4, DIFF_DIM), lambda b, h: (0, 0)),
            pl.BlockSpec((HEAD_DIM, 1), lambda b, h: (0, 0)),
        ] + w_specs,
        out_specs=[pl.BlockSpec((seq, width), lambda b, h: (b, h))] + w_specs,
        out_shape=[jax.ShapeDtypeStruct((t, GROUP_WIDTH), BF16)]
        + [jax.ShapeDtypeStruct(w.shape, BF16) for w in weights],
        scratch_shapes=[pltpu.VMEM((heads, HEAD_DIM + ONES_ROWS, seq), BF16)],
        compiler_params=_params(("arbitrary", "arbitrary")),
        name="attn_b",
    )(proj_b, proj_b, proj_b, lambda_qk, subln, *weights)
    return outs[0], outs[1:]


def _out_proj_kernel(ya_ref, yb_ref, w_ref, x_ref, o_ref, *, rows):
    for c in range(x_ref.shape[0] // rows):
        sl = pl.ds(c * rows, rows)
        acc = jnp.dot(ya_ref[sl, :], w_ref[:GROUP_WIDTH, :], preferred_element_type=F32)
        acc = acc + jnp.dot(yb_ref[sl, :], w_ref[GROUP_WIDTH:, :], preferred_element_type=F32)
        o_ref[sl, :] = x_ref[sl, :] + acc


def _out_proj(ya, yb, w_bf16, x2d, tm=512, rows=256):
    t = x2d.shape[0]
    return pl.pallas_call(
        functools.partial(_out_proj_kernel, rows=rows),
        grid=(t // tm,),
        in_specs=[
            pl.BlockSpec((tm, GROUP_WIDTH), lambda m: (m, 0)),
            pl.BlockSpec((tm, GROUP_WIDTH), lambda m: (m, 0)),
            pl.BlockSpec((2 * GROUP_WIDTH, D_MODEL), lambda m: (0, 0)),
            pl.BlockSpec((tm, D_MODEL), lambda m: (m, 0)),
        ],
        out_specs=pl.BlockSpec((tm, D_MODEL), lambda m: (m, 0)),
        out_shape=jax.ShapeDtypeStruct((t, D_MODEL), F32),
        compiler_params=_params(("arbitrary",)),
        name="out_proj",
    )(ya, yb, w_bf16, x2d)


def _ffn_kernel(x_ref, gn_ref, wg_ref, wu_ref, wd_ref, gf_ref, o_ref, h_ref, *, rows):
    j = pl.program_id(1)
    last = pl.num_programs(1) - 1
    n_chunks = x_ref.shape[0] // rows

    def chunk(c):
        return pl.ds(c * rows, rows)

    def norm(c):
        h_ref[chunk(c), :] = (_rms_scale(x_ref[chunk(c), :], RMS_EPS) * gn_ref[...]).astype(BF16)

    def run(first, final):
        acts = {}
        for i in range(n_chunks + 2):
            if first and i < n_chunks:
                norm(i)
            if 1 <= i <= n_chunks:
                h = h_ref[chunk(i - 1), :]
                gate = jnp.dot(h, wg_ref[...], preferred_element_type=F32)
                up = jnp.dot(h, wu_ref[...], preferred_element_type=F32)
                acts[i - 1] = (gate / (1.0 + jnp.exp(-gate)) * up).astype(BF16)
            if i >= 2:
                dst = chunk(i - 2)
                base = x_ref[dst, :] if first else o_ref[dst, :]
                acc = base + jnp.dot(acts.pop(i - 2), wd_ref[...], preferred_element_type=F32)
                o_ref[dst, :] = _rms_scale(acc, RMS_EPS) * gf_ref[...] if final else acc

    pl.when(j == 0)(lambda: run(True, False))
    pl.when((j > 0) & (j < last))(lambda: run(False, False))
    pl.when(j == last)(lambda: run(False, True))


def _ffn(x2d, gain, wg_bf16, wu_bf16, wd_bf16, gain_final, tm=1024, tf=512, rows=256):
    t = x2d.shape[0]
    assert D_FF // tf >= 2
    return pl.pallas_call(
        functools.partial(_ffn_kernel, rows=rows),
        grid=(t // tm, D_FF // tf),
        in_specs=[
            pl.BlockSpec((tm, D_MODEL), lambda m, j: (m, 0)),
            pl.BlockSpec((1, D_MODEL), lambda m, j: (0, 0)),
            pl.BlockSpec((D_MODEL, tf), lambda m, j: (0, j)),
            pl.BlockSpec((D_MODEL, tf), lambda m, j: (0, j)),
            pl.BlockSpec((tf, D_MODEL), lambda m, j: (j, 0)),
            pl.BlockSpec((1, D_MODEL), lambda m, j: (0, 0)),
        ],
        out_specs=pl.BlockSpec((tm, D_MODEL), lambda m, j: (m, 0)),
        out_shape=jax.ShapeDtypeStruct((t, D_MODEL), F32),
        scratch_shapes=[pltpu.VMEM((tm, D_MODEL), BF16)],
        compiler_params=_params(("arbitrary", "arbitrary")),
        name="ffn",
    )(x2d, gain, wg_bf16, wu_bf16, wd_bf16, gain_final)


def kernel(x, norm_attn, w_in, lambda_qk, subln, w_out, norm_ffn, w_gate, w_up, w_down, norm_final):
    batch, seq, d_model = x.shape
    assert d_model == D_MODEL and w_in.shape == (1, D_MODEL, 6 * GROUP_WIDTH)
    assert w_gate.shape == (1, D_MODEL, D_FF) and seq % 256 == 0
    x2d = x.reshape(batch * seq, D_MODEL)
    tabs = _rope_tables(seq)

    proj_a, proj_b = _in_proj(x2d, norm_attn[0][None, :], w_in[0].astype(BF16), tabs, seq)
    ya = _attn_a(proj_a, batch, seq)
    yb, (wo_b, wg_b, wu_b, wd_b) = _attn_b(proj_b, lambda_qk[0], subln[0][:, None],
                                           (w_out[0], w_gate[0], w_up[0], w_down[0]), batch, seq)
    x1 = _out_proj(ya, yb, wo_b, x2d)
    out = _ffn(x1, norm_ffn[0][None, :], wg_b, wu_b, wd_b, norm_final[None, :])
    return out.reshape(batch, seq, D_MODEL)
```

```python
import functools
import math

import jax
import jax.numpy as jnp
import numpy as np
from jax import lax
from jax.experimental import pallas as pl
from jax.experimental.pallas import tpu as pltpu

D_MODEL = 2048
HEAD_DIM = 128
N_HEADS = 8
GROUP_WIDTH = N_HEADS * HEAD_DIM
DIFF_DIM = HEAD_DIM // 2
D_FF = 5632
ROPE_THETA = 500000.0
RMS_EPS = 1e-6
SUBLN_EPS = 1e-5
NEG_BIAS = -1e30
HALF_WINDOW = 64
DILATIONS = (1, 4, 16)
LAM_INIT = 0.8 - 0.6 * math.exp(-0.3 * 0)
LOG2E = math.log2(math.e)

VMEM_LIMIT_BYTES = 56 * 1024 * 1024
LANES = 128

BF16 = jnp.bfloat16
F32 = jnp.float32


def _params(semantics):
    return pltpu.CompilerParams(dimension_semantics=semantics, vmem_limit_bytes=VMEM_LIMIT_BYTES)


def _rms_scale(x, eps):
    return x * lax.rsqrt(jnp.mean(x * x, axis=-1, keepdims=True) + eps)


def _rope_tables(seq):
    lane = np.arange(LANES)
    kinds = ((HEAD_DIM, HEAD_DIM // 4, HEAD_DIM ** -0.5 * LOG2E), (HEAD_DIM, HEAD_DIM // 4, 1.0),
             (DIFF_DIM, DIFF_DIM // 4, DIFF_DIM ** -0.5 * LOG2E), (DIFF_DIM, DIFF_DIM // 4, 1.0))
    pos = jnp.arange(seq, dtype=F32)[:, None]
    cos, sin, in_lo, in_hi = [], [], [], []
    for period, rot_dim, _ in kinds:
        half = rot_dim // 2
        j = lane % period
        inv_freq = ROPE_THETA ** (-jnp.arange(0, rot_dim, 2, dtype=F32) / rot_dim)
        ang = pos * inv_freq[None, :]
        cos.append(jnp.take(jnp.cos(ang), j % half, axis=1))
        sin.append(jnp.take(jnp.sin(ang), j % half, axis=1))
        in_lo.append(j < half)
        in_hi.append((j >= half) & (j < rot_dim))
    in_lo, in_hi = np.stack(in_lo)[:, None, :], np.stack(in_hi)[:, None, :]
    scale = jnp.asarray([s for _, _, s in kinds], F32)[:, None, None]
    cos, sin = jnp.stack(cos), jnp.stack(sin)
    c = jnp.where(in_lo | in_hi, cos, 1.0) * scale
    s1 = jnp.where(in_lo, -sin, 0.0) * scale
    s2 = jnp.where(in_hi, sin, 0.0) * scale
    return jnp.stack([c, s1, s2], axis=1)


def _in_proj_kernel(x_ref, g_ref, w_ref, t_ref, oa_ref, ob_ref, h_ref, *, rows):
    n = pl.program_id(1)
    n_chunks = x_ref.shape[0] // rows

    def chunk(c):
        return pl.ds(c * rows, rows)

    def norm(c):
        h_ref[chunk(c), :] = (_rms_scale(x_ref[chunk(c), :], RMS_EPS) * g_ref[...]).astype(BF16)

    def matmul(c):
        return jnp.dot(h_ref[chunk(c), :], w_ref[...], preferred_element_type=F32)

    def rope_store(out_ref, shift):
        def store(c, acc):
            cos, s1, s2 = (t_ref[0, i, chunk(c), :] for i in range(3))
            for h in range(N_HEADS):
                cols = slice(h * HEAD_DIM, (h + 1) * HEAD_DIM)
                a = acc[:, cols]
                y = a * cos + pltpu.roll(a, HEAD_DIM - shift, 1) * s1 + pltpu.roll(a, shift, 1) * s2
                out_ref[chunk(c), cols] = y.astype(out_ref.dtype)
        return store

    def plain_store(out_ref):
        def store(c, acc):
            out_ref[chunk(c), :] = acc.astype(out_ref.dtype)
        return store

    def run(store, with_norm=False):
        accs = {}
        for i in range(n_chunks + 2):
            if with_norm and i < n_chunks:
                norm(i)
            if 1 <= i <= n_chunks:
                accs[i - 1] = matmul(i - 1)
            if i >= 2:
                store(i - 2, accs.pop(i - 2))

    pl.when(n == 0)(lambda: run(rope_store(oa_ref, HEAD_DIM // 8), with_norm=True))
    pl.when(n == 1)(lambda: run(rope_store(oa_ref, HEAD_DIM // 8)))
    pl.when(n == 2)(lambda: run(plain_store(oa_ref)))
    pl.when((n == 3) | (n == 4))(lambda: run(rope_store(ob_ref, DIFF_DIM // 8)))
    pl.when(n == 5)(lambda: run(plain_store(ob_ref)))


def _in_proj(x2d, gain, w_bf16, tabs, seq, tm=1024, rows=256):
    t = x2d.shape[0]
    seq_tiles = seq // tm
    tab_idx = lambda m, n: (jnp.where(n < 2, n, jnp.clip(n - 1, 2, 3)), 0, m % seq_tiles, 0)
    tab_spec = pl.BlockSpec((1, 3, tm, LANES), tab_idx)
    return pl.pallas_call(
        functools.partial(_in_proj_kernel, rows=rows),
        grid=(t // tm, 6),
        in_specs=[
            pl.BlockSpec((tm, D_MODEL), lambda m, n: (m, 0)),
            pl.BlockSpec((1, D_MODEL), lambda m, n: (0, 0)),
            pl.BlockSpec((D_MODEL, GROUP_WIDTH), lambda m, n: (0, n)),
            tab_spec,
        ],
        out_specs=[
            pl.BlockSpec((tm, GROUP_WIDTH), lambda m, n: (m, jnp.minimum(n, 2))),
            pl.BlockSpec((tm, GROUP_WIDTH), lambda m, n: (m, jnp.maximum(n - 3, 0))),
        ],
        out_shape=[
            jax.ShapeDtypeStruct((t, 3 * GROUP_WIDTH), F32),
            jax.ShapeDtypeStruct((t, 3 * GROUP_WIDTH), BF16),
        ],
        scratch_shapes=[pltpu.VMEM((tm, D_MODEL), BF16)],
        compiler_params=_params(("arbitrary", "arbitrary")),
        name="in_proj",
    )(x2d, gain, w_bf16, tabs)


TQ_A = 128
TQ_B = 256
ONES_ROWS = 16
CAST_ROWS = 16


def _window_bias(n_q, n_k, offset):
    row = lax.broadcasted_iota(jnp.int32, (n_q, n_k), 0)
    col = lax.broadcasted_iota(jnp.int32, (n_q, n_k), 1)
    dist = col + offset - row
    return jnp.where(jnp.abs(dist) <= HALF_WINDOW, 0.0, NEG_BIAS).astype(F32)


def _attn_a_steps(q_ref, k_ref, v_ref, o_ref, st_ref, qs_ref, ks_ref, vs_ref, ob_ref, lb_ref, *, seq):
    tq = TQ_A
    n_dil = len(DILATIONS)
    assert DILATIONS == (1, 4, 16)
    len4, len16 = seq // 4, seq // 16

    for a, (src, dst) in enumerate(((q_ref, qs_ref), (k_ref, ks_ref), (v_ref, vs_ref))):
        dst[0, :, :HEAD_DIM] = src[...].astype(BF16)
        for r4 in range(4):
            part = src[pl.ds(r4, len4, stride=4), :]
            st_ref[a, pl.ds(r4 * len4, len4), :] = part
            dst[1, pl.ds(r4 * len4, len4), :HEAD_DIM] = part.astype(BF16)
        for r4 in range(4):
            for j in range(4):
                part = st_ref[a, pl.ds(r4 * len4 + j, len16, stride=4), :]
                dst[2, pl.ds((r4 + 4 * j) * len16, len16), :HEAD_DIM] = part.astype(BF16)
    vs_ref[:, :, HEAD_DIM:] = jnp.ones((n_dil, seq, HEAD_DIM), BF16)
    yield

    tiles = []
    for g, dil in enumerate(DILATIONS):
        length = seq // dil
        kw = min(256, length)
        for r in range(dil):
            for tile in range(length // tq):
                q0 = tile * tq
                ws = min(max(q0 - HALF_WINDOW, 0), length - kw)
                tiles.append((g, dil, r, r * length, q0, ws, kw))

    def scores(t):
        g, dil, r, base, q0, ws, kw = t
        qt = qs_ref[g, pl.ds(base + q0, tq), :]
        kt = ks_ref[g, pl.ds(base + ws, kw), :]
        s = lax.dot_general(qt, kt, (((1,), (1,)), ((), ())), preferred_element_type=F32)
        return s + _window_bias(tq, kw, ws - q0)

    def softmax(s):
        m = jnp.max(s, axis=-1, keepdims=True)
        return jnp.exp2(s - m).astype(BF16), m

    def finish(t, p, m):
        g, dil, r, base, q0, ws, kw = t
        ov = jnp.dot(p, vs_ref[g, pl.ds(base + ws, kw), :], preferred_element_type=F32)
        den = ov[:, HEAD_DIM:]
        out_rows = pl.ds(q0 * dil + r, tq, stride=dil) if dil > 1 else pl.ds(q0, tq)
        ob_ref[g, out_rows, :] = ov[:, :HEAD_DIM] / den
        lb_ref[g, out_rows, :] = m + jnp.log2(den)

    s_vals, p_vals = {}, {}
    for i in range(len(tiles) + 2):
        if i < len(tiles):
            s_vals[i] = scores(tiles[i])
        if 1 <= i <= len(tiles):
            p_vals[i - 1] = softmax(s_vals.pop(i - 1))
        if i >= 2:
            finish(tiles[i - 2], *p_vals.pop(i - 2))
        yield

    l0, l1, l2 = lb_ref[0], lb_ref[1], lb_ref[2]
    mx = jnp.maximum(jnp.maximum(l0, l1), l2)
    e0, e1, e2 = jnp.exp2(l0 - mx), jnp.exp2(l1 - mx), jnp.exp2(l2 - mx)
    merged = (e0 * ob_ref[0] + e1 * ob_ref[1] + e2 * ob_ref[2]) / (e0 + e1 + e2)
    o_ref[...] = merged.astype(BF16)
    yield


def _attn_b_steps(q_ref, k_ref, v_ref, lq_ref, g_ref, w_f32_refs, o_ref, w_bf16_refs, vt_ref):
    seq = q_ref.shape[0]
    tq = TQ_B
    vt_ref[:HEAD_DIM, :] = v_ref[...].astype(F32).T.astype(BF16)
    vt_ref[HEAD_DIM:, :] = jnp.ones((ONES_ROWS, seq), BF16)

    cast_jobs = [(src, dst, r0) for src, dst in zip(w_f32_refs, w_bf16_refs)
                 for r0 in range(0, src.shape[0], CAST_ROWS)]

    lq = lq_ref[...]
    lam = (jnp.exp(jnp.sum(lq[0:1] * lq[1:2], axis=-1, keepdims=True))
           - jnp.exp(jnp.sum(lq[2:3] * lq[3:4], axis=-1, keepdims=True)) + LAM_INIT)

    lane = lax.broadcasted_iota(jnp.int32, (tq, HEAD_DIM), 1)
    zero = jnp.zeros((tq, HEAD_DIM), BF16)
    yield

    def scores_t(qm):
        return lax.dot_general(k_ref[...], qm, (((1,), (1,)), ((), ())),
                               preferred_element_type=F32)

    def probs_t(st):
        return jnp.exp2(st - jnp.max(st, axis=0, keepdims=True)).astype(BF16)

    def weighted_v(pt):
        ov = jnp.dot(vt_ref[...], pt, preferred_element_type=F32)
        return ov[:HEAD_DIM] / ov[HEAD_DIM:HEAD_DIM + 1]

    n_tiles = seq // tq
    n_iters = n_tiles + 2
    scores, probs = {}, {}
    for i in range(n_iters):
        for src, dst, r0 in cast_jobs[i::n_iters]:
            dst[pl.ds(r0, CAST_ROWS), :] = src[pl.ds(r0, CAST_ROWS), :].astype(BF16)
        if i < n_tiles:
            q = q_ref[i * tq:(i + 1) * tq, :]
            scores[i] = (scores_t(jnp.where(lane < DIFF_DIM, q, zero)),
                         scores_t(jnp.where(lane >= DIFF_DIM, q, zero)))
        if 1 <= i <= n_tiles:
            st1, st2 = scores.pop(i - 1)
            probs[i - 1] = (probs_t(st1), probs_t(st2))
        if i >= 2:
            pt1, pt2 = probs.pop(i - 2)
            yt = weighted_v(pt1) - lam * weighted_v(pt2)
            yt = yt * lax.rsqrt(jnp.mean(yt * yt, axis=0, keepdims=True) + SUBLN_EPS)
            yt = yt * g_ref[...] * (1.0 - LAM_INIT)
            o_ref[(i - 2) * tq:(i - 1) * tq, :] = yt.T.astype(BF16)
        yield


def _head_block(seq, group):
    return pl.BlockSpec((seq, HEAD_DIM), lambda b, h: (b, group * N_HEADS + h))


def _attn_a_kernel(*refs, seq):
    for _ in _attn_a_steps(*refs, seq=seq):
        pass


def _attn_a(proj_a, batch, seq):
    n_dil = len(DILATIONS)
    return pl.pallas_call(
        functools.partial(_attn_a_kernel, seq=seq),
        grid=(batch, N_HEADS),
        in_specs=[_head_block(seq, 0), _head_block(seq, 1), _head_block(seq, 2)],
        out_specs=_head_block(seq, 0),
        out_shape=jax.ShapeDtypeStruct((batch * seq, GROUP_WIDTH), BF16),
        scratch_shapes=[
            pltpu.VMEM((3, seq, HEAD_DIM), F32),
            pltpu.VMEM((n_dil, seq, HEAD_DIM), BF16),
            pltpu.VMEM((n_dil, seq, HEAD_DIM), BF16),
            pltpu.VMEM((n_dil, seq, 2 * HEAD_DIM), BF16),
            pltpu.VMEM((n_dil, seq, HEAD_DIM), F32),
            pltpu.VMEM((n_dil, seq, LANES), F32),
        ],
        compiler_params=_params(("arbitrary", "arbitrary")),
        name="attn_a",
    )(proj_a, proj_a, proj_a)


def _attn_b_kernel(*refs, n_weights):
    q_ref, k_ref, v_ref, lq_ref, g_ref = refs[:5]
    w_f32_refs = refs[5:5 + n_weights]
    o_ref = refs[5 + n_weights]
    w_bf16_refs = refs[6 + n_weights:6 + 2 * n_weights]
    vt_ref = refs[6 + 2 * n_weights]
    for _ in _attn_b_steps(q_ref, k_ref, v_ref, lq_ref, g_ref, w_f32_refs, o_ref, w_bf16_refs, vt_ref):
        pass


def _attn_b(proj_b, lambda_qk, subln, weights, batch, seq):
    steps = batch * N_HEADS
    w_specs = []
    for w in weights:
        rows = w.shape[0] // steps
        assert rows * steps == w.shape[0] and rows % CAST_ROWS == 0
        w_specs.append(pl.BlockSpec((rows, w.shape[1]), lambda b, h: (b * N_HEADS + h, 0)))
    outs = pl.pallas_call(
        functools.partial(_attn_b_kernel, n_weights=len(weights)),
        grid=(batch, N_HEADS),
        in_specs=[_head_block(seq, 0), _head_block(seq, 1), _head_block(seq, 2),
                  pl.BlockSpec((4, DIFF_DIM), lambda b, h: (0, 0)),
                  pl.BlockSpec((HEAD_DIM, 1), lambda b, h: (0, 0))] + w_specs,
        out_specs=[_head_block(seq, 0)] + w_specs,
        out_shape=[jax.ShapeDtypeStruct((batch * seq, GROUP_WIDTH), BF16)]
        + [jax.ShapeDtypeStruct(w.shape, BF16) for w in weights],
        scratch_shapes=[pltpu.VMEM((HEAD_DIM + ONES_ROWS, seq), BF16)],
        compiler_params=_params(("arbitrary", "arbitrary")),
        name="attn_b",
    )(proj_b, proj_b, proj_b, lambda_qk, subln, *weights)
    return outs[0], outs[1:]


def _out_proj_kernel(ya_ref, yb_ref, w_ref, x_ref, o_ref, *, rows):
    for c in range(x_ref.shape[0] // rows):
        sl = pl.ds(c * rows, rows)
        acc = jnp.dot(ya_ref[sl, :], w_ref[:GROUP_WIDTH, :], preferred_element_type=F32)
        acc = acc + jnp.dot(yb_ref[sl, :], w_ref[GROUP_WIDTH:, :], preferred_element_type=F32)
        o_ref[sl, :] = x_ref[sl, :] + acc


def _out_proj(ya, yb, w_bf16, x2d, tm=512, rows=256):
    t = x2d.shape[0]
    return pl.pallas_call(
        functools.partial(_out_proj_kernel, rows=rows),
        grid=(t // tm,),
        in_specs=[
            pl.BlockSpec((tm, GROUP_WIDTH), lambda m: (m, 0)),
            pl.BlockSpec((tm, GROUP_WIDTH), lambda m: (m, 0)),
            pl.BlockSpec((2 * GROUP_WIDTH, D_MODEL), lambda m: (0, 0)),
            pl.BlockSpec((tm, D_MODEL), lambda m: (m, 0)),
        ],
        out_specs=pl.BlockSpec((tm, D_MODEL), lambda m: (m, 0)),
        out_shape=jax.ShapeDtypeStruct((t, D_MODEL), F32),
        compiler_params=_params(("arbitrary",)),
        name="out_proj",
    )(ya, yb, w_bf16, x2d)


def _ffn_kernel(x_ref, gn_ref, wg_ref, wu_ref, wd_ref, gf_ref, o_ref, h_ref, *, rows):
    j = pl.program_id(1)
    last = pl.num_programs(1) - 1
    n_chunks = x_ref.shape[0] // rows

    def chunk(c):
        return pl.ds(c * rows, rows)

    def norm(c):
        h_ref[chunk(c), :] = (_rms_scale(x_ref[chunk(c), :], RMS_EPS) * gn_ref[...]).astype(BF16)

    def run(first, final):
        acts = {}
        for i in range(n_chunks + 2):
            if first and i < n_chunks:
                norm(i)
            if 1 <= i <= n_chunks:
                h = h_ref[chunk(i - 1), :]
                gate = jnp.dot(h, wg_ref[...], preferred_element_type=F32)
                up = jnp.dot(h, wu_ref[...], preferred_element_type=F32)
                acts[i - 1] = (gate / (1.0 + jnp.exp(-gate)) * up).astype(BF16)
            if i >= 2:
                dst = chunk(i - 2)
                base = x_ref[dst, :] if first else o_ref[dst, :]
                acc = base + jnp.dot(acts.pop(i - 2), wd_ref[...], preferred_element_type=F32)
                o_ref[dst, :] = _rms_scale(acc, RMS_EPS) * gf_ref[...] if final else acc

    pl.when(j == 0)(lambda: run(True, False))
    pl.when((j > 0) & (j < last))(lambda: run(False, False))
    pl.when(j == last)(lambda: run(False, True))


def _ffn(x2d, gain, wg_bf16, wu_bf16, wd_bf16, gain_final, tm=1024, tf=512, rows=256):
    t = x2d.shape[0]
    assert D_FF // tf >= 2
    return pl.pallas_call(
        functools.partial(_ffn_kernel, rows=rows),
        grid=(t // tm, D_FF // tf),
        in_specs=[
            pl.BlockSpec((tm, D_MODEL), lambda m, j: (m, 0)),
            pl.BlockSpec((1, D_MODEL), lambda m, j: (0, 0)),
            pl.BlockSpec((D_MODEL, tf), lambda m, j: (0, j)),
            pl.BlockSpec((D_MODEL, tf), lambda m, j: (0, j)),
            pl.BlockSpec((tf, D_MODEL), lambda m, j: (j, 0)),
            pl.BlockSpec((1, D_MODEL), lambda m, j: (0, 0)),
        ],
        out_specs=pl.BlockSpec((tm, D_MODEL), lambda m, j: (m, 0)),
        out_shape=jax.ShapeDtypeStruct((t, D_MODEL), F32),
        scratch_shapes=[pltpu.VMEM((tm, D_MODEL), BF16)],
        compiler_params=_params(("arbitrary", "arbitrary")),
        name="ffn",
    )(x2d, gain, wg_bf16, wu_bf16, wd_bf16, gain_final)


def kernel(x, norm_attn, w_in, lambda_qk, subln, w_out, norm_ffn, w_gate, w_up, w_down, norm_final):
    batch, seq, d_model = x.shape
    assert d_model == D_MODEL and w_in.shape == (1, D_MODEL, 6 * GROUP_WIDTH)
    assert w_gate.shape == (1, D_MODEL, D_FF) and seq % 256 == 0
    x2d = x.reshape(batch * seq, D_MODEL)
    tabs = _rope_tables(seq)

    proj_a, proj_b = _in_proj(x2d, norm_attn[0][None, :], w_in[0].astype(BF16), tabs, seq)
    ya = _attn_a(proj_a, batch, seq)
    yb, (wo_b, wg_b, wu_b, wd_b) = _attn_b(proj_b, lambda_qk[0], subln[0][:, None],
                                           (w_out[0], w_gate[0], w_up[0], w_down[0]), batch, seq)
    x1 = _out_proj(ya, yb, wo_b, x2d)
    out = _ffn(x1, norm_ffn[0][None, :], wg_b, wu_b, wd_b, norm_final[None, :])
    return out.reshape(batch, seq, D_MODEL)
```

```python
import functools
import math

import jax
import jax.numpy as jnp
import numpy as np
from jax import lax
from jax.experimental import pallas as pl
from jax.experimental.pallas import tpu as pltpu

D_MODEL = 2048
HEAD_DIM = 128
N_HEADS = 8
GROUP_WIDTH = N_HEADS * HEAD_DIM
DIFF_DIM = HEAD_DIM // 2
D_FF = 5632
ROPE_THETA = 500000.0
RMS_EPS = 1e-6
SUBLN_EPS = 1e-5
NEG_BIAS = -1e30
HALF_WINDOW = 64
DILATIONS = (1, 4, 16)
LAM_INIT = 0.8 - 0.6 * math.exp(-0.3 * 0)
LOG2E = math.log2(math.e)

VMEM_LIMIT_BYTES = 56 * 1024 * 1024
LANES = 128

BF16 = jnp.bfloat16
F32 = jnp.float32


def _params(semantics):
    return pltpu.CompilerParams(dimension_semantics=semantics, vmem_limit_bytes=VMEM_LIMIT_BYTES)


def _rms_scale(x, eps):
    return x * lax.rsqrt(jnp.mean(x * x, axis=-1, keepdims=True) + eps)


def _rope_tables(seq):
    lane = np.arange(LANES)
    kinds = ((HEAD_DIM, HEAD_DIM // 4, HEAD_DIM ** -0.5 * LOG2E), (HEAD_DIM, HEAD_DIM // 4, 1.0),
             (DIFF_DIM, DIFF_DIM // 4, DIFF_DIM ** -0.5 * LOG2E), (DIFF_DIM, DIFF_DIM // 4, 1.0))
    pos = jnp.arange(seq, dtype=F32)[:, None]
    cos, sin, in_lo, in_hi = [], [], [], []
    for period, rot_dim, _ in kinds:
        half = rot_dim // 2
        j = lane % period
        inv_freq = ROPE_THETA ** (-jnp.arange(0, rot_dim, 2, dtype=F32) / rot_dim)
        ang = pos * inv_freq[None, :]
        cos.append(jnp.take(jnp.cos(ang), j % half, axis=1))
        sin.append(jnp.take(jnp.sin(ang), j % half, axis=1))
        in_lo.append(j < half)
        in_hi.append((j >= half) & (j < rot_dim))
    in_lo, in_hi = np.stack(in_lo)[:, None, :], np.stack(in_hi)[:, None, :]
    scale = jnp.asarray([s for _, _, s in kinds], F32)[:, None, None]
    cos, sin = jnp.stack(cos), jnp.stack(sin)
    c = jnp.where(in_lo | in_hi, cos, 1.0) * scale
    s1 = jnp.where(in_lo, -sin, 0.0) * scale
    s2 = jnp.where(in_hi, sin, 0.0) * scale
    return jnp.stack([c, s1, s2], axis=1)


def _in_proj_kernel(x_ref, g_ref, w_ref, t_ref, oa_ref, ob_ref, h_ref, *, rows):
    n = pl.program_id(1)
    n_chunks = x_ref.shape[0] // rows

    def chunk(c):
        return pl.ds(c * rows, rows)

    def norm(c):
        h_ref[chunk(c), :] = (_rms_scale(x_ref[chunk(c), :], RMS_EPS) * g_ref[...]).astype(BF16)

    def matmul(c):
        return jnp.dot(h_ref[chunk(c), :], w_ref[...], preferred_element_type=F32)

    def rope_store(out_ref, shift):
        def store(c, acc):
            cos, s1, s2 = (t_ref[0, i, chunk(c), :] for i in range(3))
            for h in range(N_HEADS):
                cols = slice(h * HEAD_DIM, (h + 1) * HEAD_DIM)
                a = acc[:, cols]
                y = a * cos + pltpu.roll(a, HEAD_DIM - shift, 1) * s1 + pltpu.roll(a, shift, 1) * s2
                out_ref[h, chunk(c), :] = y.astype(out_ref.dtype)
        return store

    def plain_store(out_ref):
        def store(c, acc):
            for h in range(N_HEADS):
                out_ref[h, chunk(c), :] = acc[:, h * HEAD_DIM:(h + 1) * HEAD_DIM].astype(out_ref.dtype)
        return store

    def run(store, with_norm=False):
        accs = {}
        for i in range(n_chunks + 2):
            if with_norm and i < n_chunks:
                norm(i)
            if 1 <= i <= n_chunks:
                accs[i - 1] = matmul(i - 1)
            if i >= 2:
                store(i - 2, accs.pop(i - 2))

    pl.when(n == 0)(lambda: run(rope_store(oa_ref, HEAD_DIM // 8), with_norm=True))
    pl.when(n == 1)(lambda: run(rope_store(oa_ref, HEAD_DIM // 8)))
    pl.when(n == 2)(lambda: run(plain_store(oa_ref)))
    pl.when((n == 3) | (n == 4))(lambda: run(rope_store(ob_ref, DIFF_DIM // 8)))
    pl.when(n == 5)(lambda: run(plain_store(ob_ref)))


def _in_proj(x2d, gain, w_bf16, tabs, seq, tm=1024, rows=256):
    t = x2d.shape[0]
    seq_tiles = seq // tm
    tab_idx = lambda m, n: (jnp.where(n < 2, n, jnp.clip(n - 1, 2, 3)), 0, m % seq_tiles, 0)
    tab_spec = pl.BlockSpec((1, 3, tm, LANES), tab_idx)
    return pl.pallas_call(
        functools.partial(_in_proj_kernel, rows=rows),
        grid=(t // tm, 6),
        in_specs=[
            pl.BlockSpec((tm, D_MODEL), lambda m, n: (m, 0)),
            pl.BlockSpec((1, D_MODEL), lambda m, n: (0, 0)),
            pl.BlockSpec((D_MODEL, GROUP_WIDTH), lambda m, n: (0, n)),
            tab_spec,
        ],
        out_specs=[
            pl.BlockSpec((N_HEADS, tm, HEAD_DIM), lambda m, n: (jnp.minimum(n, 2), m, 0)),
            pl.BlockSpec((N_HEADS, tm, HEAD_DIM), lambda m, n: (jnp.maximum(n - 3, 0), m, 0)),
        ],
        out_shape=[
            jax.ShapeDtypeStruct((3 * N_HEADS, t, HEAD_DIM), F32),
            jax.ShapeDtypeStruct((3 * N_HEADS, t, HEAD_DIM), BF16),
        ],
        scratch_shapes=[pltpu.VMEM((tm, D_MODEL), BF16)],
        compiler_params=_params(("arbitrary", "arbitrary")),
        name="in_proj",
    )(x2d, gain, w_bf16, tabs)


TQ_A = 128
TQ_B = 256
ONES_ROWS = 16
CAST_ROWS = 16


def _window_bias(n_q, n_k, offset):
    row = lax.broadcasted_iota(jnp.int32, (n_q, n_k), 0)
    col = lax.broadcasted_iota(jnp.int32, (n_q, n_k), 1)
    dist = col + offset - row
    return jnp.where(jnp.abs(dist) <= HALF_WINDOW, 0.0, NEG_BIAS).astype(F32)


def _attn_a_steps(q_ref, k_ref, v_ref, o_ref, st_ref, qs_ref, ks_ref, vs_ref, ob_ref, lb_ref, *, seq):
    tq = TQ_A
    n_dil = len(DILATIONS)
    assert DILATIONS == (1, 4, 16)
    len4, len16 = seq // 4, seq // 16

    for a, (src, dst) in enumerate(((q_ref, qs_ref), (k_ref, ks_ref), (v_ref, vs_ref))):
        dst[0, :, :HEAD_DIM] = src[...].astype(BF16)
        for r4 in range(4):
            part = src[pl.ds(r4, len4, stride=4), :]
            st_ref[a, pl.ds(r4 * len4, len4), :] = part
            dst[1, pl.ds(r4 * len4, len4), :HEAD_DIM] = part.astype(BF16)
        for r4 in range(4):
            for j in range(4):
                part = st_ref[a, pl.ds(r4 * len4 + j, len16, stride=4), :]
                dst[2, pl.ds((r4 + 4 * j) * len16, len16), :HEAD_DIM] = part.astype(BF16)
    vs_ref[:, :, HEAD_DIM:] = jnp.ones((n_dil, seq, HEAD_DIM), BF16)
    yield

    tiles = []
    for g, dil in enumerate(DILATIONS):
        length = seq // dil
        kw = min(256, length)
        for r in range(dil):
            for tile in range(length // tq):
                q0 = tile * tq
                ws = min(max(q0 - HALF_WINDOW, 0), length - kw)
                tiles.append((g, dil, r, r * length, q0, ws, kw))

    def scores(t):
        g, dil, r, base, q0, ws, kw = t
        qt = qs_ref[g, pl.ds(base + q0, tq), :]
        kt = ks_ref[g, pl.ds(base + ws, kw), :]
        s = lax.dot_general(qt, kt, (((1,), (1,)), ((), ())), preferred_element_type=F32)
        return s + _window_bias(tq, kw, ws - q0)

    def softmax(s):
        m = jnp.max(s, axis=-1, keepdims=True)
        return jnp.exp2(s - m).astype(BF16), m

    def finish(t, p, m):
        g, dil, r, base, q0, ws, kw = t
        ov = jnp.dot(p, vs_ref[g, pl.ds(base + ws, kw), :], preferred_element_type=F32)
        den = ov[:, HEAD_DIM:]
        out_rows = pl.ds(q0 * dil + r, tq, stride=dil) if dil > 1 else pl.ds(q0, tq)
        ob_ref[g, out_rows, :] = ov[:, :HEAD_DIM] / den
        lb_ref[g, out_rows, :] = m + jnp.log2(den)

    s_vals, p_vals = {}, {}
    for i in range(len(tiles) + 2):
        if i < len(tiles):
            s_vals[i] = scores(tiles[i])
        if 1 <= i <= len(tiles):
            p_vals[i - 1] = softmax(s_vals.pop(i - 1))
        if i >= 2:
            finish(tiles[i - 2], *p_vals.pop(i - 2))
        yield

    l0, l1, l2 = lb_ref[0], lb_ref[1], lb_ref[2]
    mx = jnp.maximum(jnp.maximum(l0, l1), l2)
    e0, e1, e2 = jnp.exp2(l0 - mx), jnp.exp2(l1 - mx), jnp.exp2(l2 - mx)
    merged = (e0 * ob_ref[0] + e1 * ob_ref[1] + e2 * ob_ref[2]) / (e0 + e1 + e2)
    o_ref[...] = merged.astype(BF16)
    yield


def _attn_b_steps(q_ref, k_ref, v_ref, lq_ref, g_ref, w_f32_refs, o_ref, w_bf16_refs, vt_ref):
    seq = q_ref.shape[0]
    tq = TQ_B
    vt_ref[:HEAD_DIM, :] = v_ref[...].astype(F32).T.astype(BF16)
    vt_ref[HEAD_DIM:, :] = jnp.ones((ONES_ROWS, seq), BF16)

    cast_jobs = [(src, dst, r0) for src, dst in zip(w_f32_refs, w_bf16_refs)
                 for r0 in range(0, src.shape[0], CAST_ROWS)]

    lq = lq_ref[...]
    lam = (jnp.exp(jnp.sum(lq[0:1] * lq[1:2], axis=-1, keepdims=True))
           - jnp.exp(jnp.sum(lq[2:3] * lq[3:4], axis=-1, keepdims=True)) + LAM_INIT)

    lane = lax.broadcasted_iota(jnp.int32, (tq, HEAD_DIM), 1)
    zero = jnp.zeros((tq, HEAD_DIM), BF16)
    yield

    def scores_t(qm):
        st = lax.dot_general(k_ref[...], qm, (((1,), (1,)), ((), ())),
                             preferred_element_type=F32)
        return st.astype(BF16), jnp.max(st, axis=0, keepdims=True)

    def probs_t(staged):
        sb, m = staged
        return jnp.exp2(sb.astype(F32) - m).astype(BF16)

    def weighted_v(pt):
        ov = jnp.dot(vt_ref[...], pt, preferred_element_type=F32)
        return ov[:HEAD_DIM] / ov[HEAD_DIM:HEAD_DIM + 1]

    n_tiles = seq // tq
    n_iters = n_tiles + 2
    scores, probs = {}, {}
    for i in range(n_iters):
        for src, dst, r0 in cast_jobs[i::n_iters]:
            dst[pl.ds(r0, CAST_ROWS), :] = src[pl.ds(r0, CAST_ROWS), :].astype(BF16)
        if i < n_tiles:
            q = q_ref[i * tq:(i + 1) * tq, :]
            scores[i] = (scores_t(jnp.where(lane < DIFF_DIM, q, zero)),
                         scores_t(jnp.where(lane >= DIFF_DIM, q, zero)))
        if 1 <= i <= n_tiles:
            st1, st2 = scores.pop(i - 1)
            probs[i - 1] = (probs_t(st1), probs_t(st2))
        if i >= 2:
            pt1, pt2 = probs.pop(i - 2)
            yt = weighted_v(pt1) - lam * weighted_v(pt2)
            yt = yt * lax.rsqrt(jnp.mean(yt * yt, axis=0, keepdims=True) + SUBLN_EPS)
            yt = yt * g_ref[...] * (1.0 - LAM_INIT)
            o_ref[(i - 2) * tq:(i - 1) * tq, :] = yt.T.astype(BF16)
        yield


def _head_block(seq, group):
    return pl.BlockSpec((None, seq, HEAD_DIM), lambda b, h: (group * N_HEADS + h, b, 0))


def _attn_a_kernel(*refs, seq):
    for _ in _attn_a_steps(*refs, seq=seq):
        pass


def _attn_a(proj_a, batch, seq):
    n_dil = len(DILATIONS)
    return pl.pallas_call(
        functools.partial(_attn_a_kernel, seq=seq),
        grid=(batch, N_HEADS),
        in_specs=[_head_block(seq, 0), _head_block(seq, 1), _head_block(seq, 2)],
        out_specs=_head_block(seq, 0),
        out_shape=jax.ShapeDtypeStruct((N_HEADS, batch * seq, HEAD_DIM), BF16),
        scratch_shapes=[
            pltpu.VMEM((3, seq, HEAD_DIM), F32),
            pltpu.VMEM((n_dil, seq, HEAD_DIM), BF16),
            pltpu.VMEM((n_dil, seq, HEAD_DIM), BF16),
            pltpu.VMEM((n_dil, seq, 2 * HEAD_DIM), BF16),
            pltpu.VMEM((n_dil, seq, HEAD_DIM), F32),
            pltpu.VMEM((n_dil, seq, LANES), F32),
        ],
        compiler_params=_params(("arbitrary", "arbitrary")),
        name="attn_a",
    )(proj_a, proj_a, proj_a)


def _attn_b_kernel(*refs, n_weights):
    q_ref, k_ref, v_ref, lq_ref, g_ref = refs[:5]
    w_f32_refs = refs[5:5 + n_weights]
    o_ref = refs[5 + n_weights]
    w_bf16_refs = refs[6 + n_weights:6 + 2 * n_weights]
    vt_ref = refs[6 + 2 * n_weights]
    for _ in _attn_b_steps(q_ref, k_ref, v_ref, lq_ref, g_ref, w_f32_refs, o_ref, w_bf16_refs, vt_ref):
        pass


def _attn_b(proj_b, lambda_qk, subln, weights, batch, seq):
    steps = batch * N_HEADS
    w_specs = []
    for w in weights:
        rows = w.shape[0] // steps
        assert rows * steps == w.shape[0] and rows % CAST_ROWS == 0
        w_specs.append(pl.BlockSpec((rows, w.shape[1]), lambda b, h: (b * N_HEADS + h, 0)))
    outs = pl.pallas_call(
        functools.partial(_attn_b_kernel, n_weights=len(weights)),
        grid=(batch, N_HEADS),
        in_specs=[_head_block(seq, 0), _head_block(seq, 1), _head_block(seq, 2),
                  pl.BlockSpec((4, DIFF_DIM), lambda b, h: (0, 0)),
                  pl.BlockSpec((HEAD_DIM, 1), lambda b, h: (0, 0))] + w_specs,
        out_specs=[_head_block(seq, 0)] + w_specs,
        out_shape=[jax.ShapeDtypeStruct((N_HEADS, batch * seq, HEAD_DIM), BF16)]
        + [jax.ShapeDtypeStruct(w.shape, BF16) for w in weights],
        scratch_shapes=[pltpu.VMEM((HEAD_DIM + ONES_ROWS, seq), BF16)],
        compiler_params=_params(("arbitrary", "arbitrary")),
        name="attn_b",
    )(proj_b, proj_b, proj_b, lambda_qk, subln, *weights)
    return outs[0], outs[1:]


def _out_proj_kernel(ya_ref, yb_ref, w_ref, x_ref, o_ref, *, rows):
    for c in range(x_ref.shape[0] // rows):
        sl = pl.ds(c * rows, rows)
        y = jnp.concatenate([ref[h, sl, :] for ref in (ya_ref, yb_ref) for h in range(N_HEADS)], axis=1)
        o_ref[sl, :] = x_ref[sl, :] + jnp.dot(y, w_ref[...], preferred_element_type=F32)


def _out_proj(ya, yb, w_bf16, x2d, tm=512, rows=256):
    t = x2d.shape[0]
    return pl.pallas_call(
        functools.partial(_out_proj_kernel, rows=rows),
        grid=(t // tm,),
        in_specs=[
            pl.BlockSpec((N_HEADS, tm, HEAD_DIM), lambda m: (0, m, 0)),
            pl.BlockSpec((N_HEADS, tm, HEAD_DIM), lambda m: (0, m, 0)),
            pl.BlockSpec((2 * GROUP_WIDTH, D_MODEL), lambda m: (0, 0)),
            pl.BlockSpec((tm, D_MODEL), lambda m: (m, 0)),
        ],
        out_specs=pl.BlockSpec((tm, D_MODEL), lambda m: (m, 0)),
        out_shape=jax.ShapeDtypeStruct((t, D_MODEL), F32),
        compiler_params=_params(("arbitrary",)),
        name="out_proj",
    )(ya, yb, w_bf16, x2d)


def _ffn_kernel(x_ref, gn_ref, wg_ref, wu_ref, wd_ref, gf_ref, o_ref, h_ref, *, rows):
    j = pl.program_id(1)
    last = pl.num_programs(1) - 1
    n_chunks = x_ref.shape[0] // rows

    def chunk(c):
        return pl.ds(c * rows, rows)

    def norm(c):
        h_ref[chunk(c), :] = (_rms_scale(x_ref[chunk(c), :], RMS_EPS) * gn_ref[...]).astype(BF16)

    def run(first, final):
        acts = {}
        for i in range(n_chunks + 2):
            if first and i < n_chunks:
                norm(i)
            if 1 <= i <= n_chunks:
                h = h_ref[chunk(i - 1), :]
                gate = jnp.dot(h, wg_ref[...], preferred_element_type=F32)
                up = jnp.dot(h, wu_ref[...], preferred_element_type=F32)
                acts[i - 1] = (gate / (1.0 + jnp.exp(-gate)) * up).astype(BF16)
            if i >= 2:
                dst = chunk(i - 2)
                base = x_ref[dst, :] if first else o_ref[dst, :]
                acc = base + jnp.dot(acts.pop(i - 2), wd_ref[...], preferred_element_type=F32)
                o_ref[dst, :] = _rms_scale(acc, RMS_EPS) * gf_ref[...] if final else acc

    pl.when(j == 0)(lambda: run(True, False))
    pl.when((j > 0) & (j < last))(lambda: run(False, False))
    pl.when(j == last)(lambda: run(False, True))


def _ffn(x2d, gain, wg_bf16, wu_bf16, wd_bf16, gain_final, tm=1024, tf=512, rows=256):
    t = x2d.shape[0]
    assert D_FF // tf >= 2
    return pl.pallas_call(
        functools.partial(_ffn_kernel, rows=rows),
        grid=(t // tm, D_FF // tf),
        in_specs=[
            pl.BlockSpec((tm, D_MODEL), lambda m, j: (m, 0)),
            pl.BlockSpec((1, D_MODEL), lambda m, j: (0, 0)),
            pl.BlockSpec((D_MODEL, tf), lambda m, j: (0, j)),
            pl.BlockSpec((D_MODEL, tf), lambda m, j: (0, j)),
            pl.BlockSpec((tf, D_MODEL), lambda m, j: (j, 0)),
            pl.BlockSpec((1, D_MODEL), lambda m, j: (0, 0)),
        ],
        out_specs=pl.BlockSpec((tm, D_MODEL), lambda m, j: (m, 0)),
        out_shape=jax.ShapeDtypeStruct((t, D_MODEL), F32),
        scratch_shapes=[pltpu.VMEM((tm, D_MODEL), BF16)],
        compiler_params=_params(("arbitrary", "arbitrary")),
        name="ffn",
    )(x2d, gain, wg_bf16, wu_bf16, wd_bf16, gain_final)


def kernel(x, norm_attn, w_in, lambda_qk, subln, w_out, norm_ffn, w_gate, w_up, w_down, norm_final):
    batch, seq, d_model = x.shape
    assert d_model == D_MODEL and w_in.shape == (1, D_MODEL, 6 * GROUP_WIDTH)
    assert w_gate.shape == (1, D_MODEL, D_FF) and seq % 256 == 0
    x2d = x.reshape(batch * seq, D_MODEL)
    tabs = _rope_tables(seq)

    proj_a, proj_b = _in_proj(x2d, norm_attn[0][None, :], w_in[0].astype(BF16), tabs, seq)
    ya = _attn_a(proj_a, batch, seq)
    yb, (wo_b, wg_b, wu_b, wd_b) = _attn_b(proj_b, lambda_qk[0], subln[0][:, None],
                                           (w_out[0], w_gate[0], w_up[0], w_down[0]), batch, seq)
    x1 = _out_proj(ya, yb, wo_b, x2d)
    out = _ffn(x1, norm_ffn[0][None, :], wg_b, wu_b, wd_b, norm_final[None, :])
    return out.reshape(batch, seq, D_MODEL)
```

```python
import functools
import math

import jax
import jax.numpy as jnp
import numpy as np
from jax import lax
from jax.experimental import pallas as pl
from jax.experimental.pallas import tpu as pltpu

D_MODEL = 2048
HEAD_DIM = 128
N_HEADS = 8
GROUP_WIDTH = N_HEADS * HEAD_DIM
DIFF_DIM = HEAD_DIM // 2
D_FF = 5632
ROPE_THETA = 500000.0
RMS_EPS = 1e-6
SUBLN_EPS = 1e-5
NEG_BIAS = -1e30
HALF_WINDOW = 64
DILATIONS = (1, 4, 16)
LAM_INIT = 0.8 - 0.6 * math.exp(-0.3 * 0)
LOG2E = math.log2(math.e)

VMEM_LIMIT_BYTES = 56 * 1024 * 1024
LANES = 128

BF16 = jnp.bfloat16
F32 = jnp.float32


def _params(semantics):
    return pltpu.CompilerParams(dimension_semantics=semantics, vmem_limit_bytes=VMEM_LIMIT_BYTES)


def _rms_scale(x, eps):
    return x * lax.rsqrt(jnp.mean(x * x, axis=-1, keepdims=True) + eps)


def _rope_tables(seq):
    lane = np.arange(LANES)
    kinds = ((HEAD_DIM, HEAD_DIM // 4, HEAD_DIM ** -0.5 * LOG2E), (HEAD_DIM, HEAD_DIM // 4, 1.0),
             (DIFF_DIM, DIFF_DIM // 4, DIFF_DIM ** -0.5 * LOG2E), (DIFF_DIM, DIFF_DIM // 4, 1.0))
    pos = jnp.arange(seq, dtype=F32)[:, None]
    cos, sin, in_lo, in_hi = [], [], [], []
    for period, rot_dim, _ in kinds:
        half = rot_dim // 2
        j = lane % period
        inv_freq = ROPE_THETA ** (-jnp.arange(0, rot_dim, 2, dtype=F32) / rot_dim)
        ang = pos * inv_freq[None, :]
        cos.append(jnp.take(jnp.cos(ang), j % half, axis=1))
        sin.append(jnp.take(jnp.sin(ang), j % half, axis=1))
        in_lo.append(j < half)
        in_hi.append((j >= half) & (j < rot_dim))
    in_lo, in_hi = np.stack(in_lo)[:, None, :], np.stack(in_hi)[:, None, :]
    scale = jnp.asarray([s for _, _, s in kinds], F32)[:, None, None]
    cos, sin = jnp.stack(cos), jnp.stack(sin)
    c = jnp.where(in_lo | in_hi, cos, 1.0) * scale
    s1 = jnp.where(in_lo, -sin, 0.0) * scale
    s2 = jnp.where(in_hi, sin, 0.0) * scale
    return jnp.stack([c, s1, s2], axis=1)


def _in_proj_kernel(x_ref, g_ref, w_ref, t_ref, oa_ref, ob_ref, h_ref, *, rows):
    n = pl.program_id(1)
    n_chunks = x_ref.shape[0] // rows

    def chunk(c):
        return pl.ds(c * rows, rows)

    def norm(c):
        h_ref[chunk(c), :] = (_rms_scale(x_ref[chunk(c), :], RMS_EPS) * g_ref[...]).astype(BF16)

    def matmul(c):
        return jnp.dot(h_ref[chunk(c), :], w_ref[...], preferred_element_type=F32)

    def rope_store(out_ref, shift):
        def store(c, acc):
            cos, s1, s2 = (t_ref[0, i, chunk(c), :] for i in range(3))
            for h in range(N_HEADS):
                cols = slice(h * HEAD_DIM, (h + 1) * HEAD_DIM)
                a = acc[:, cols]
                y = a * cos + pltpu.roll(a, HEAD_DIM - shift, 1) * s1 + pltpu.roll(a, shift, 1) * s2
                out_ref[h, chunk(c), :] = y.astype(out_ref.dtype)
        return store

    def plain_store(out_ref):
        def store(c, acc):
            for h in range(N_HEADS):
                out_ref[h, chunk(c), :] = acc[:, h * HEAD_DIM:(h + 1) * HEAD_DIM].astype(out_ref.dtype)
        return store

    def run(store, with_norm=False):
        accs = {}
        for i in range(n_chunks + 2):
            if with_norm and i < n_chunks:
                norm(i)
            if 1 <= i <= n_chunks:
                accs[i - 1] = matmul(i - 1)
            if i >= 2:
                store(i - 2, accs.pop(i - 2))

    pl.when(n == 0)(lambda: run(rope_store(oa_ref, HEAD_DIM // 8), with_norm=True))
    pl.when(n == 1)(lambda: run(rope_store(oa_ref, HEAD_DIM // 8)))
    pl.when(n == 2)(lambda: run(plain_store(oa_ref)))
    pl.when((n == 3) | (n == 4))(lambda: run(rope_store(ob_ref, DIFF_DIM // 8)))
    pl.when(n == 5)(lambda: run(plain_store(ob_ref)))


def _in_proj(x2d, gain, w_bf16, tabs, seq, tm=1024, rows=256):
    t = x2d.shape[0]
    seq_tiles = seq // tm
    tab_idx = lambda m, n: (jnp.where(n < 2, n, jnp.clip(n - 1, 2, 3)), 0, m % seq_tiles, 0)
    tab_spec = pl.BlockSpec((1, 3, tm, LANES), tab_idx)
    return pl.pallas_call(
        functools.partial(_in_proj_kernel, rows=rows),
        grid=(t // tm, 6),
        in_specs=[
            pl.BlockSpec((tm, D_MODEL), lambda m, n: (m, 0)),
            pl.BlockSpec((1, D_MODEL), lambda m, n: (0, 0)),
            pl.BlockSpec((D_MODEL, GROUP_WIDTH), lambda m, n: (0, n)),
            tab_spec,
        ],
        out_specs=[
            pl.BlockSpec((N_HEADS, tm, HEAD_DIM), lambda m, n: (jnp.minimum(n, 2), m, 0)),
            pl.BlockSpec((N_HEADS, tm, HEAD_DIM), lambda m, n: (jnp.maximum(n - 3, 0), m, 0)),
        ],
        out_shape=[
            jax.ShapeDtypeStruct((3 * N_HEADS, t, HEAD_DIM), F32),
            jax.ShapeDtypeStruct((3 * N_HEADS, t, HEAD_DIM), BF16),
        ],
        scratch_shapes=[pltpu.VMEM((tm, D_MODEL), BF16)],
        compiler_params=_params(("arbitrary", "arbitrary")),
        name="in_proj",
    )(x2d, gain, w_bf16, tabs)


TQ_A = 128
TQ_B = 256
ONES_ROWS = 16
CAST_ROWS = 16


def _window_bias(n_q, n_k, offset):
    row = lax.broadcasted_iota(jnp.int32, (n_q, n_k), 0)
    col = lax.broadcasted_iota(jnp.int32, (n_q, n_k), 1)
    dist = col + offset - row
    return jnp.where(jnp.abs(dist) <= HALF_WINDOW, 0.0, NEG_BIAS).astype(F32)


def _attn_a_steps(q_ref, k_ref, v_ref, o_ref, st_ref, qs_ref, ks_ref, vs_ref, ob_ref, lb_ref, *, seq):
    tq = TQ_A
    n_dil = len(DILATIONS)
    assert DILATIONS == (1, 4, 16)
    len4, len16 = seq // 4, seq // 16

    for a, (src, dst) in enumerate(((q_ref, qs_ref), (k_ref, ks_ref), (v_ref, vs_ref))):
        dst[0, :, :HEAD_DIM] = src[...].astype(BF16)
        for r4 in range(4):
            part = src[pl.ds(r4, len4, stride=4), :]
            st_ref[a, pl.ds(r4 * len4, len4), :] = part
            dst[1, pl.ds(r4 * len4, len4), :HEAD_DIM] = part.astype(BF16)
        for r4 in range(4):
            for j in range(4):
                part = st_ref[a, pl.ds(r4 * len4 + j, len16, stride=4), :]
                dst[2, pl.ds((r4 + 4 * j) * len16, len16), :HEAD_DIM] = part.astype(BF16)
    vs_ref[:, :, HEAD_DIM:] = jnp.ones((n_dil, seq, HEAD_DIM), BF16)
    yield

    tiles = []
    for g, dil in list(enumerate(DILATIONS))[1:] + list(enumerate(DILATIONS))[:1]:
        length = seq // dil
        kw = min(256, length)
        for r in range(dil):
            for tile in range(length // tq):
                q0 = tile * tq
                ws = min(max(q0 - HALF_WINDOW, 0), length - kw)
                tiles.append((g, dil, r, r * length, q0, ws, kw))

    def scores(t):
        g, dil, r, base, q0, ws, kw = t
        qt = qs_ref[g, pl.ds(base + q0, tq), :]
        kt = ks_ref[g, pl.ds(base + ws, kw), :]
        s = lax.dot_general(qt, kt, (((1,), (1,)), ((), ())), preferred_element_type=F32)
        return s + _window_bias(tq, kw, ws - q0)

    def softmax(s):
        m = jnp.max(s, axis=-1, keepdims=True)
        return jnp.exp2(s - m).astype(BF16), m

    def finish(t, p, m):
        g, dil, r, base, q0, ws, kw = t
        ov = jnp.dot(p, vs_ref[g, pl.ds(base + ws, kw), :], preferred_element_type=F32)
        den = ov[:, HEAD_DIM:]
        o0, l0 = ov[:, :HEAD_DIM] / den, m + jnp.log2(den)
        if dil > 1:
            out_rows = pl.ds(q0 * dil + r, tq, stride=dil)
            ob_ref[g - 1, out_rows, :] = o0
            lb_ref[g - 1, out_rows, :] = l0
            return
        rows = pl.ds(q0, tq)
        l1, l2 = lb_ref[0, rows, :], lb_ref[1, rows, :]
        mx = jnp.maximum(jnp.maximum(l0, l1), l2)
        e0, e1, e2 = jnp.exp2(l0 - mx), jnp.exp2(l1 - mx), jnp.exp2(l2 - mx)
        merged = (e0 * o0 + e1 * ob_ref[0, rows, :] + e2 * ob_ref[1, rows, :]) / (e0 + e1 + e2)
        o_ref[rows, :] = merged.astype(BF16)

    s_vals, p_vals = {}, {}
    for i in range(len(tiles) + 2):
        if i < len(tiles):
            s_vals[i] = scores(tiles[i])
        if 1 <= i <= len(tiles):
            p_vals[i - 1] = softmax(s_vals.pop(i - 1))
        if i >= 2:
            finish(tiles[i - 2], *p_vals.pop(i - 2))
        yield


def _attn_b_steps(q_ref, k_ref, v_ref, lq_ref, g_ref, w_f32_refs, o_ref, w_bf16_refs, vt_ref):
    seq = q_ref.shape[0]
    tq = TQ_B
    vt_ref[:HEAD_DIM, :] = v_ref[...].astype(F32).T.astype(BF16)
    vt_ref[HEAD_DIM:, :] = jnp.ones((ONES_ROWS, seq), BF16)

    cast_jobs = [(src, dst, r0) for src, dst in zip(w_f32_refs, w_bf16_refs)
                 for r0 in range(0, src.shape[0], CAST_ROWS)]

    lq = lq_ref[...]
    lam = (jnp.exp(jnp.sum(lq[0:1] * lq[1:2], axis=-1, keepdims=True))
           - jnp.exp(jnp.sum(lq[2:3] * lq[3:4], axis=-1, keepdims=True)) + LAM_INIT)

    lane = lax.broadcasted_iota(jnp.int32, (tq, HEAD_DIM), 1)
    zero = jnp.zeros((tq, HEAD_DIM), BF16)
    yield

    def scores_t(qm):
        st = lax.dot_general(k_ref[...], qm, (((1,), (1,)), ((), ())),
                             preferred_element_type=F32)
        return st.astype(BF16), jnp.max(st, axis=0, keepdims=True)

    def probs_t(staged):
        sb, m = staged
        return jnp.exp2(sb.astype(F32) - m).astype(BF16)

    def weighted_v(pt):
        ov = jnp.dot(vt_ref[...], pt, preferred_element_type=F32)
        return ov[:HEAD_DIM] / ov[HEAD_DIM:HEAD_DIM + 1]

    n_tiles = seq // tq
    n_iters = n_tiles + 2
    scores, probs = {}, {}
    for i in range(n_iters):
        for src, dst, r0 in cast_jobs[i::n_iters]:
            dst[pl.ds(r0, CAST_ROWS), :] = src[pl.ds(r0, CAST_ROWS), :].astype(BF16)
        if i < n_tiles:
            q = q_ref[i * tq:(i + 1) * tq, :]
            scores[i] = (scores_t(jnp.where(lane < DIFF_DIM, q, zero)),
                         scores_t(jnp.where(lane >= DIFF_DIM, q, zero)))
        if 1 <= i <= n_tiles:
            st1, st2 = scores.pop(i - 1)
            probs[i - 1] = (probs_t(st1), probs_t(st2))
        if i >= 2:
            pt1, pt2 = probs.pop(i - 2)
            yt = weighted_v(pt1) - lam * weighted_v(pt2)
            yt = yt * lax.rsqrt(jnp.mean(yt * yt, axis=0, keepdims=True) + SUBLN_EPS)
            yt = yt * g_ref[...] * (1.0 - LAM_INIT)
            o_ref[(i - 2) * tq:(i - 1) * tq, :] = yt.T.astype(BF16)
        yield


def _head_block(seq, group):
    return pl.BlockSpec((None, seq, HEAD_DIM), lambda b, h: (group * N_HEADS + h, b, 0))


def _attn_a_kernel(*refs, seq):
    for _ in _attn_a_steps(*refs, seq=seq):
        pass


def _attn_a(proj_a, batch, seq):
    n_dil = len(DILATIONS)
    return pl.pallas_call(
        functools.partial(_attn_a_kernel, seq=seq),
        grid=(batch, N_HEADS),
        in_specs=[_head_block(seq, 0), _head_block(seq, 1), _head_block(seq, 2)],
        out_specs=_head_block(seq, 0),
        out_shape=jax.ShapeDtypeStruct((N_HEADS, batch * seq, HEAD_DIM), BF16),
        scratch_shapes=[
            pltpu.VMEM((3, seq, HEAD_DIM), F32),
            pltpu.VMEM((n_dil, seq, HEAD_DIM), BF16),
            pltpu.VMEM((n_dil, seq, HEAD_DIM), BF16),
            pltpu.VMEM((n_dil, seq, 2 * HEAD_DIM), BF16),
            pltpu.VMEM((n_dil - 1, seq, HEAD_DIM), F32),
            pltpu.VMEM((n_dil - 1, seq, LANES), F32),
        ],
        compiler_params=_params(("arbitrary", "arbitrary")),
        name="attn_a",
    )(proj_a, proj_a, proj_a)


def _attn_b_kernel(*refs, n_weights):
    q_ref, k_ref, v_ref, lq_ref, g_ref = refs[:5]
    w_f32_refs = refs[5:5 + n_weights]
    o_ref = refs[5 + n_weights]
    w_bf16_refs = refs[6 + n_weights:6 + 2 * n_weights]
    vt_ref = refs[6 + 2 * n_weights]
    for _ in _attn_b_steps(q_ref, k_ref, v_ref, lq_ref, g_ref, w_f32_refs, o_ref, w_bf16_refs, vt_ref):
        pass


def _attn_b(proj_b, lambda_qk, subln, weights, batch, seq):
    steps = batch * N_HEADS
    w_specs = []
    for w in weights:
        rows = w.shape[0] // steps
        assert rows * steps == w.shape[0] and rows % CAST_ROWS == 0
        w_specs.append(pl.BlockSpec((rows, w.shape[1]), lambda b, h: (b * N_HEADS + h, 0)))
    outs = pl.pallas_call(
        functools.partial(_attn_b_kernel, n_weights=len(weights)),
        grid=(batch, N_HEADS),
        in_specs=[_head_block(seq, 0), _head_block(seq, 1), _head_block(seq, 2),
                  pl.BlockSpec((4, DIFF_DIM), lambda b, h: (0, 0)),
                  pl.BlockSpec((HEAD_DIM, 1), lambda b, h: (0, 0))] + w_specs,
        out_specs=[_head_block(seq, 0)] + w_specs,
        out_shape=[jax.ShapeDtypeStruct((N_HEADS, batch * seq, HEAD_DIM), BF16)]
        + [jax.ShapeDtypeStruct(w.shape, BF16) for w in weights],
        scratch_shapes=[pltpu.VMEM((HEAD_DIM + ONES_ROWS, seq), BF16)],
        compiler_params=_params(("arbitrary", "arbitrary")),
        name="attn_b",
    )(proj_b, proj_b, proj_b, lambda_qk, subln, *weights)
    return outs[0], outs[1:]


def _out_proj_kernel(ya_ref, yb_ref, w_ref, x_ref, o_ref, *, rows):
    for c in range(x_ref.shape[0] // rows):
        sl = pl.ds(c * rows, rows)
        y = jnp.concatenate([ref[h, sl, :] for ref in (ya_ref, yb_ref) for h in range(N_HEADS)], axis=1)
        o_ref[sl, :] = x_ref[sl, :] + jnp.dot(y, w_ref[...], preferred_element_type=F32)


def _out_proj(ya, yb, w_bf16, x2d, tm=512, rows=256):
    t = x2d.shape[0]
    return pl.pallas_call(
        functools.partial(_out_proj_kernel, rows=rows),
        grid=(t // tm,),
        in_specs=[
            pl.BlockSpec((N_HEADS, tm, HEAD_DIM), lambda m: (0, m, 0)),
            pl.BlockSpec((N_HEADS, tm, HEAD_DIM), lambda m: (0, m, 0)),
            pl.BlockSpec((2 * GROUP_WIDTH, D_MODEL), lambda m: (0, 0)),
            pl.BlockSpec((tm, D_MODEL), lambda m: (m, 0)),
        ],
        out_specs=pl.BlockSpec((tm, D_MODEL), lambda m: (m, 0)),
        out_shape=jax.ShapeDtypeStruct((t, D_MODEL), F32),
        compiler_params=_params(("arbitrary",)),
        name="out_proj",
    )(ya, yb, w_bf16, x2d)


def _ffn_kernel(x_ref, gn_ref, wg_ref, wu_ref, wd_ref, gf_ref, o_ref, h_ref, *, rows):
    j = pl.program_id(1)
    last = pl.num_programs(1) - 1
    n_chunks = x_ref.shape[0] // rows

    def chunk(c):
        return pl.ds(c * rows, rows)

    def norm(c):
        h_ref[chunk(c), :] = (_rms_scale(x_ref[chunk(c), :], RMS_EPS) * gn_ref[...]).astype(BF16)

    def run(first, final):
        acts = {}
        for i in range(n_chunks + 2):
            if first and i < n_chunks:
                norm(i)
            if 1 <= i <= n_chunks:
                h = h_ref[chunk(i - 1), :]
                gate = jnp.dot(h, wg_ref[...], preferred_element_type=F32)
                up = jnp.dot(h, wu_ref[...], preferred_element_type=F32)
                acts[i - 1] = (gate / (1.0 + jnp.exp(-gate)) * up).astype(BF16)
            if i >= 2:
                dst = chunk(i - 2)
                base = x_ref[dst, :] if first else o_ref[dst, :]
                acc = base + jnp.dot(acts.pop(i - 2), wd_ref[...], preferred_element_type=F32)
                o_ref[dst, :] = _rms_scale(acc, RMS_EPS) * gf_ref[...] if final else acc

    pl.when(j == 0)(lambda: run(True, False))
    pl.when((j > 0) & (j < last))(lambda: run(False, False))
    pl.when(j == last)(lambda: run(False, True))


def _ffn(x2d, gain, wg_bf16, wu_bf16, wd_bf16, gain_final, tm=1024, tf=512, rows=256):
    t = x2d.shape[0]
    assert D_FF // tf >= 2
    return pl.pallas_call(
        functools.partial(_ffn_kernel, rows=rows),
        grid=(t // tm, D_FF // tf),
        in_specs=[
            pl.BlockSpec((tm, D_MODEL), lambda m, j: (m, 0)),
            pl.BlockSpec((1, D_MODEL), lambda m, j: (0, 0)),
            pl.BlockSpec((D_MODEL, tf), lambda m, j: (0, j)),
            pl.BlockSpec((D_MODEL, tf), lambda m, j: (0, j)),
            pl.BlockSpec((tf, D_MODEL), lambda m, j: (j, 0)),
            pl.BlockSpec((1, D_MODEL), lambda m, j: (0, 0)),
        ],
        out_specs=pl.BlockSpec((tm, D_MODEL), lambda m, j: (m, 0)),
        out_shape=jax.ShapeDtypeStruct((t, D_MODEL), F32),
        scratch_shapes=[pltpu.VMEM((tm, D_MODEL), BF16)],
        compiler_params=_params(("arbitrary", "arbitrary")),
        name="ffn",
    )(x2d, gain, wg_bf16, wu_bf16, wd_bf16, gain_final)


def kernel(x, norm_attn, w_in, lambda_qk, subln, w_out, norm_ffn, w_gate, w_up, w_down, norm_final):
    batch, seq, d_model = x.shape
    assert d_model == D_MODEL and w_in.shape == (1, D_MODEL, 6 * GROUP_WIDTH)
    assert w_gate.shape == (1, D_MODEL, D_FF) and seq % 256 == 0
    x2d = x.reshape(batch * seq, D_MODEL)
    tabs = _rope_tables(seq)

    proj_a, proj_b = _in_proj(x2d, norm_attn[0][None, :], w_in[0].astype(BF16), tabs, seq)
    ya = _attn_a(proj_a, batch, seq)
    yb, (wo_b, wg_b, wu_b, wd_b) = _attn_b(proj_b, lambda_qk[0], subln[0][:, None],
                                           (w_out[0], w_gate[0], w_up[0], w_down[0]), batch, seq)
    x1 = _out_proj(ya, yb, wo_b, x2d)
    out = _ffn(x1, norm_ffn[0][None, :], wg_b, wu_b, wd_b, norm_final[None, :])
    return out.reshape(batch, seq, D_MODEL)
```

```python
import functools
import math

import jax
import jax.numpy as jnp
import numpy as np
from jax import lax
from jax.experimental import pallas as pl
from jax.experimental.pallas import tpu as pltpu

D_MODEL = 2048
HEAD_DIM = 128
N_HEADS = 8
GROUP_WIDTH = N_HEADS * HEAD_DIM
DIFF_DIM = HEAD_DIM // 2
D_FF = 5632
ROPE_THETA = 500000.0
RMS_EPS = 1e-6
SUBLN_EPS = 1e-5
NEG_BIAS = -1e30
HALF_WINDOW = 64
DILATIONS = (1, 4, 16)
LAM_INIT = 0.8 - 0.6 * math.exp(-0.3 * 0)
LOG2E = math.log2(math.e)

VMEM_LIMIT_BYTES = 56 * 1024 * 1024
LANES = 128

BF16 = jnp.bfloat16
F32 = jnp.float32


def _params(semantics):
    return pltpu.CompilerParams(dimension_semantics=semantics, vmem_limit_bytes=VMEM_LIMIT_BYTES)


def _rms_scale(x, eps):
    return x * lax.rsqrt(jnp.mean(x * x, axis=-1, keepdims=True) + eps)


def _rope_tables(seq):
    lane = np.arange(LANES)
    kinds = ((HEAD_DIM, HEAD_DIM // 4, HEAD_DIM ** -0.5 * LOG2E), (HEAD_DIM, HEAD_DIM // 4, 1.0),
             (DIFF_DIM, DIFF_DIM // 4, DIFF_DIM ** -0.5 * LOG2E), (DIFF_DIM, DIFF_DIM // 4, 1.0))
    pos = jnp.arange(seq, dtype=F32)[:, None]
    cos, sin, in_lo, in_hi = [], [], [], []
    for period, rot_dim, _ in kinds:
        half = rot_dim // 2
        j = lane % period
        inv_freq = ROPE_THETA ** (-jnp.arange(0, rot_dim, 2, dtype=F32) / rot_dim)
        ang = pos * inv_freq[None, :]
        cos.append(jnp.take(jnp.cos(ang), j % half, axis=1))
        sin.append(jnp.take(jnp.sin(ang), j % half, axis=1))
        in_lo.append(j < half)
        in_hi.append((j >= half) & (j < rot_dim))
    in_lo, in_hi = np.stack(in_lo)[:, None, :], np.stack(in_hi)[:, None, :]
    scale = jnp.asarray([s for _, _, s in kinds], F32)[:, None, None]
    cos, sin = jnp.stack(cos), jnp.stack(sin)
    c = jnp.where(in_lo | in_hi, cos, 1.0) * scale
    s1 = jnp.where(in_lo, -sin, 0.0) * scale
    s2 = jnp.where(in_hi, sin, 0.0) * scale
    return jnp.stack([c, s1, s2], axis=1)


def _in_proj_kernel(x_ref, g_ref, w_ref, t_ref, oa_ref, ob_ref, h_ref, *, rows):
    n = pl.program_id(1)
    n_chunks = x_ref.shape[0] // rows

    def chunk(c):
        return pl.ds(c * rows, rows)

    def norm(c):
        h_ref[chunk(c), :] = (_rms_scale(x_ref[chunk(c), :], RMS_EPS) * g_ref[...]).astype(BF16)

    def matmul(c):
        return jnp.dot(h_ref[chunk(c), :], w_ref[...], preferred_element_type=F32)

    def rope_store(out_ref, shift):
        def store(c, acc):
            cos, s1, s2 = (t_ref[0, i, chunk(c), :] for i in range(3))
            for h in range(N_HEADS):
                cols = slice(h * HEAD_DIM, (h + 1) * HEAD_DIM)
                a = acc[:, cols]
                y = a * cos + pltpu.roll(a, HEAD_DIM - shift, 1) * s1 + pltpu.roll(a, shift, 1) * s2
                out_ref[h, chunk(c), :] = y.astype(out_ref.dtype)
        return store

    def plain_store(out_ref):
        def store(c, acc):
            for h in range(N_HEADS):
                out_ref[h, chunk(c), :] = acc[:, h * HEAD_DIM:(h + 1) * HEAD_DIM].astype(out_ref.dtype)
        return store

    def run(store, with_norm=False):
        accs = {}
        for i in range(n_chunks + 2):
            if with_norm and i < n_chunks:
                norm(i)
            if 1 <= i <= n_chunks:
                accs[i - 1] = matmul(i - 1)
            if i >= 2:
                store(i - 2, accs.pop(i - 2))

    pl.when(n == 0)(lambda: run(rope_store(oa_ref, HEAD_DIM // 8), with_norm=True))
    pl.when(n == 1)(lambda: run(rope_store(oa_ref, HEAD_DIM // 8)))
    pl.when(n == 2)(lambda: run(plain_store(oa_ref)))
    pl.when((n == 3) | (n == 4))(lambda: run(rope_store(ob_ref, DIFF_DIM // 8)))
    pl.when(n == 5)(lambda: run(plain_store(ob_ref)))


def _in_proj(x2d, gain, w_bf16, tabs, seq, tm=1024, rows=256):
    t = x2d.shape[0]
    seq_tiles = seq // tm
    tab_idx = lambda m, n: (jnp.where(n < 2, n, jnp.clip(n - 1, 2, 3)), 0, m % seq_tiles, 0)
    tab_spec = pl.BlockSpec((1, 3, tm, LANES), tab_idx)
    return pl.pallas_call(
        functools.partial(_in_proj_kernel, rows=rows),
        grid=(t // tm, 6),
        in_specs=[
            pl.BlockSpec((tm, D_MODEL), lambda m, n: (m, 0)),
            pl.BlockSpec((1, D_MODEL), lambda m, n: (0, 0)),
            pl.BlockSpec((D_MODEL, GROUP_WIDTH), lambda m, n: (0, n)),
            tab_spec,
        ],
        out_specs=[
            pl.BlockSpec((N_HEADS, tm, HEAD_DIM), lambda m, n: (jnp.minimum(n, 2), m, 0)),
            pl.BlockSpec((N_HEADS, tm, HEAD_DIM), lambda m, n: (jnp.maximum(n - 3, 0), m, 0)),
        ],
        out_shape=[
            jax.ShapeDtypeStruct((3 * N_HEADS, t, HEAD_DIM), F32),
            jax.ShapeDtypeStruct((3 * N_HEADS, t, HEAD_DIM), BF16),
        ],
        scratch_shapes=[pltpu.VMEM((tm, D_MODEL), BF16)],
        compiler_params=_params(("arbitrary", "arbitrary")),
        name="in_proj",
    )(x2d, gain, w_bf16, tabs)


TQ_A = 128
TQ_B = 256
ONES_ROWS = 16
CAST_ROWS = 16


def _window_bias(n_q, n_k, offset):
    row = lax.broadcasted_iota(jnp.int32, (n_q, n_k), 0)
    col = lax.broadcasted_iota(jnp.int32, (n_q, n_k), 1)
    dist = col + offset - row
    return jnp.where(jnp.abs(dist) <= HALF_WINDOW, 0.0, NEG_BIAS).astype(F32)


def _attn_a_steps(q_ref, k_ref, v_ref, o_ref, st_ref, qs_ref, ks_ref, vs_ref, ob_ref, lb_ref, *, seq):
    tq = TQ_A
    n_dil = len(DILATIONS)
    assert DILATIONS == (1, 4, 16)
    len4, len16 = seq // 4, seq // 16

    for a, (src, dst) in enumerate(((q_ref, qs_ref), (k_ref, ks_ref), (v_ref, vs_ref))):
        dst[0, :, :HEAD_DIM] = src[...].astype(BF16)
        for r4 in range(4):
            part = src[pl.ds(r4, len4, stride=4), :]
            st_ref[a, pl.ds(r4 * len4, len4), :] = part
            dst[1, pl.ds(r4 * len4, len4), :HEAD_DIM] = part.astype(BF16)
        for r4 in range(4):
            for j in range(4):
                part = st_ref[a, pl.ds(r4 * len4 + j, len16, stride=4), :]
                dst[2, pl.ds((r4 + 4 * j) * len16, len16), :HEAD_DIM] = part.astype(BF16)
    vs_ref[:, :, HEAD_DIM:] = jnp.ones((n_dil, seq, HEAD_DIM), BF16)
    yield

    tiles = []
    for g, dil in list(enumerate(DILATIONS))[1:] + list(enumerate(DILATIONS))[:1]:
        length = seq // dil
        kw = min(256, length)
        for r in range(dil):
            for tile in range(length // tq):
                q0 = tile * tq
                ws = min(max(q0 - HALF_WINDOW, 0), length - kw)
                tiles.append((g, dil, r, r * length, q0, ws, kw))

    def scores(t):
        g, dil, r, base, q0, ws, kw = t
        qt = qs_ref[g, pl.ds(base + q0, tq), :]
        kt = ks_ref[g, pl.ds(base + ws, kw), :]
        s = lax.dot_general(qt, kt, (((1,), (1,)), ((), ())), preferred_element_type=F32)
        return s + _window_bias(tq, kw, ws - q0)

    def softmax(s):
        m = jnp.max(s, axis=-1, keepdims=True)
        return jnp.exp2(s - m).astype(BF16), m

    def finish(t, p, m):
        g, dil, r, base, q0, ws, kw = t
        ov = jnp.dot(p, vs_ref[g, pl.ds(base + ws, kw), :], preferred_element_type=F32)
        den = ov[:, HEAD_DIM:]
        o0, l0 = ov[:, :HEAD_DIM] / den, m + jnp.log2(den)
        if dil > 1:
            out_rows = pl.ds(q0 * dil + r, tq, stride=dil)
            ob_ref[g - 1, out_rows, :] = o0
            lb_ref[g - 1, out_rows, :] = l0
            return
        rows = pl.ds(q0, tq)
        l1, l2 = lb_ref[0, rows, :], lb_ref[1, rows, :]
        mx = jnp.maximum(jnp.maximum(l0, l1), l2)
        e0, e1, e2 = jnp.exp2(l0 - mx), jnp.exp2(l1 - mx), jnp.exp2(l2 - mx)
        merged = (e0 * o0 + e1 * ob_ref[0, rows, :] + e2 * ob_ref[1, rows, :]) / (e0 + e1 + e2)
        o_ref[rows, :] = merged.astype(BF16)

    s_vals, p_vals = {}, {}
    for i in range(len(tiles) + 2):
        if i < len(tiles):
            s_vals[i] = scores(tiles[i])
        if 1 <= i <= len(tiles):
            p_vals[i - 1] = softmax(s_vals.pop(i - 1))
        if i >= 2:
            finish(tiles[i - 2], *p_vals.pop(i - 2))
        yield


def _attn_b_steps(q_ref, k_ref, v_ref, lq_ref, g_ref, w_f32_refs, o_ref, w_bf16_refs, vt_ref):
    seq = q_ref.shape[0]
    tq = TQ_B
    vt_ref[:HEAD_DIM, :] = v_ref[...].astype(F32).T.astype(BF16)
    vt_ref[HEAD_DIM:, :] = jnp.ones((ONES_ROWS, seq), BF16)

    cast_jobs = [(src, dst, r0) for src, dst in zip(w_f32_refs, w_bf16_refs)
                 for r0 in range(0, src.shape[0], CAST_ROWS)]

    lq = lq_ref[...]
    lam = (jnp.exp(jnp.sum(lq[0:1] * lq[1:2], axis=-1, keepdims=True))
           - jnp.exp(jnp.sum(lq[2:3] * lq[3:4], axis=-1, keepdims=True)) + LAM_INIT)

    lane = lax.broadcasted_iota(jnp.int32, (tq, HEAD_DIM), 1)
    zero = jnp.zeros((tq, HEAD_DIM), BF16)
    yield

    def scores_t(qm):
        st = lax.dot_general(k_ref[...], qm, (((1,), (1,)), ((), ())),
                             preferred_element_type=F32)
        return st.astype(BF16), jnp.max(st, axis=0, keepdims=True)

    def probs_t(staged):
        sb, m = staged
        return jnp.exp2(sb.astype(F32) - m).astype(BF16)

    def weighted_v(pt):
        ov = jnp.dot(vt_ref[...], pt, preferred_element_type=F32)
        return ov[:HEAD_DIM] / ov[HEAD_DIM:HEAD_DIM + 1]

    n_tiles = seq // tq
    n_iters = n_tiles + 2
    scores, probs = {}, {}
    for i in range(n_iters):
        for src, dst, r0 in cast_jobs[i::n_iters]:
            dst[pl.ds(r0, CAST_ROWS), :] = src[pl.ds(r0, CAST_ROWS), :].astype(BF16)
        if i < n_tiles:
            q = q_ref[i * tq:(i + 1) * tq, :]
            scores[i] = (scores_t(jnp.where(lane < DIFF_DIM, q, zero)),
                         scores_t(jnp.where(lane >= DIFF_DIM, q, zero)))
        if 1 <= i <= n_tiles:
            st1, st2 = scores.pop(i - 1)
            probs[i - 1] = (probs_t(st1), probs_t(st2))
        if i >= 2:
            pt1, pt2 = probs.pop(i - 2)
            yt = weighted_v(pt1) - lam * weighted_v(pt2)
            yt = yt * lax.rsqrt(jnp.mean(yt * yt, axis=0, keepdims=True) + SUBLN_EPS)
            yt = yt * g_ref[...] * (1.0 - LAM_INIT)
            o_ref[(i - 2) * tq:(i - 1) * tq, :] = yt.T.astype(BF16)
        yield


def _head_block(seq, group):
    return pl.BlockSpec((None, seq, HEAD_DIM), lambda b, h: (group * N_HEADS + h, b, 0))


def _attn_a_kernel(*refs, seq, heads):
    in_refs, o_ref, scratch = refs[:3 * heads], refs[3 * heads], refs[3 * heads + 1:]
    for hd in range(heads):
        own = [ref.at[hd] for ref in scratch]
        for _ in _attn_a_steps(*in_refs[3 * hd:3 * hd + 3], o_ref.at[hd], *own, seq=seq):
            pass


def _attn_a(proj_a, batch, seq, heads=2):
    n_dil = len(DILATIONS)
    in_specs = [pl.BlockSpec((None, seq, HEAD_DIM),
                             lambda b, h, g=g, hd=hd: (g * N_HEADS + h * heads + hd, b, 0))
                for hd in range(heads) for g in range(3)]
    return pl.pallas_call(
        functools.partial(_attn_a_kernel, seq=seq, heads=heads),
        grid=(batch, N_HEADS // heads),
        in_specs=in_specs,
        out_specs=pl.BlockSpec((heads, seq, HEAD_DIM), lambda b, h: (h, b, 0)),
        out_shape=jax.ShapeDtypeStruct((N_HEADS, batch * seq, HEAD_DIM), BF16),
        scratch_shapes=[
            pltpu.VMEM((heads, 3, seq, HEAD_DIM), F32),
            pltpu.VMEM((heads, n_dil, seq, HEAD_DIM), BF16),
            pltpu.VMEM((heads, n_dil, seq, HEAD_DIM), BF16),
            pltpu.VMEM((heads, n_dil, seq, 2 * HEAD_DIM), BF16),
            pltpu.VMEM((heads, n_dil - 1, seq, HEAD_DIM), F32),
            pltpu.VMEM((heads, n_dil - 1, seq, LANES), F32),
        ],
        compiler_params=_params(("arbitrary", "arbitrary")),
        name="attn_a",
    )(*([proj_a] * (3 * heads)))


def _attn_b_kernel(*refs, n_weights):
    q_ref, k_ref, v_ref, lq_ref, g_ref = refs[:5]
    w_f32_refs = refs[5:5 + n_weights]
    o_ref = refs[5 + n_weights]
    w_bf16_refs = refs[6 + n_weights:6 + 2 * n_weights]
    vt_ref = refs[6 + 2 * n_weights]
    for _ in _attn_b_steps(q_ref, k_ref, v_ref, lq_ref, g_ref, w_f32_refs, o_ref, w_bf16_refs, vt_ref):
        pass


def _attn_b(proj_b, lambda_qk, subln, weights, batch, seq):
    steps = batch * N_HEADS
    w_specs = []
    for w in weights:
        rows = w.shape[0] // steps
        assert rows * steps == w.shape[0] and rows % CAST_ROWS == 0
        w_specs.append(pl.BlockSpec((rows, w.shape[1]), lambda b, h: (b * N_HEADS + h, 0)))
    outs = pl.pallas_call(
        functools.partial(_attn_b_kernel, n_weights=len(weights)),
        grid=(batch, N_HEADS),
        in_specs=[_head_block(seq, 0), _head_block(seq, 1), _head_block(seq, 2),
                  pl.BlockSpec((4, DIFF_DIM), lambda b, h: (0, 0)),
                  pl.BlockSpec((HEAD_DIM, 1), lambda b, h: (0, 0))] + w_specs,
        out_specs=[_head_block(seq, 0)] + w_specs,
        out_shape=[jax.ShapeDtypeStruct((N_HEADS, batch * seq, HEAD_DIM), BF16)]
        + [jax.ShapeDtypeStruct(w.shape, BF16) for w in weights],
        scratch_shapes=[pltpu.VMEM((HEAD_DIM + ONES_ROWS, seq), BF16)],
        compiler_params=_params(("arbitrary", "arbitrary")),
        name="attn_b",
    )(proj_b, proj_b, proj_b, lambda_qk, subln, *weights)
    return outs[0], outs[1:]


def _out_proj_kernel(ya_ref, yb_ref, w_ref, x_ref, o_ref, *, rows):
    for c in range(x_ref.shape[0] // rows):
        sl = pl.ds(c * rows, rows)
        y = jnp.concatenate([ref[h, sl, :] for ref in (ya_ref, yb_ref) for h in range(N_HEADS)], axis=1)
        o_ref[sl, :] = x_ref[sl, :] + jnp.dot(y, w_ref[...], preferred_element_type=F32)


def _out_proj(ya, yb, w_bf16, x2d, tm=512, rows=256):
    t = x2d.shape[0]
    return pl.pallas_call(
        functools.partial(_out_proj_kernel, rows=rows),
        grid=(t // tm,),
        in_specs=[
            pl.BlockSpec((N_HEADS, tm, HEAD_DIM), lambda m: (0, m, 0)),
            pl.BlockSpec((N_HEADS, tm, HEAD_DIM), lambda m: (0, m, 0)),
            pl.BlockSpec((2 * GROUP_WIDTH, D_MODEL), lambda m: (0, 0)),
            pl.BlockSpec((tm, D_MODEL), lambda m: (m, 0)),
        ],
        out_specs=pl.BlockSpec((tm, D_MODEL), lambda m: (m, 0)),
        out_shape=jax.ShapeDtypeStruct((t, D_MODEL), F32),
        compiler_params=_params(("arbitrary",)),
        name="out_proj",
    )(ya, yb, w_bf16, x2d)


def _ffn_kernel(x_ref, gn_ref, wg_ref, wu_ref, wd_ref, gf_ref, o_ref, h_ref, *, rows):
    j = pl.program_id(1)
    last = pl.num_programs(1) - 1
    n_chunks = x_ref.shape[0] // rows

    def chunk(c):
        return pl.ds(c * rows, rows)

    def norm(c):
        h_ref[chunk(c), :] = (_rms_scale(x_ref[chunk(c), :], RMS_EPS) * gn_ref[...]).astype(BF16)

    def run(first, final):
        acts = {}
        for i in range(n_chunks + 2):
            if first and i < n_chunks:
                norm(i)
            if 1 <= i <= n_chunks:
                h = h_ref[chunk(i - 1), :]
                gate = jnp.dot(h, wg_ref[...], preferred_element_type=F32)
                up = jnp.dot(h, wu_ref[...], preferred_element_type=F32)
                acts[i - 1] = (gate / (1.0 + jnp.exp(-gate)) * up).astype(BF16)
            if i >= 2:
                dst = chunk(i - 2)
                base = x_ref[dst, :] if first else o_ref[dst, :]
                acc = base + jnp.dot(acts.pop(i - 2), wd_ref[...], preferred_element_type=F32)
                o_ref[dst, :] = _rms_scale(acc, RMS_EPS) * gf_ref[...] if final else acc

    pl.when(j == 0)(lambda: run(True, False))
    pl.when((j > 0) & (j < last))(lambda: run(False, False))
    pl.when(j == last)(lambda: run(False, True))


def _ffn(x2d, gain, wg_bf16, wu_bf16, wd_bf16, gain_final, tm=1024, tf=512, rows=256):
    t = x2d.shape[0]
    assert D_FF // tf >= 2
    return pl.pallas_call(
        functools.partial(_ffn_kernel, rows=rows),
        grid=(t // tm, D_FF // tf),
        in_specs=[
            pl.BlockSpec((tm, D_MODEL), lambda m, j: (m, 0)),
            pl.BlockSpec((1, D_MODEL), lambda m, j: (0, 0)),
            pl.BlockSpec((D_MODEL, tf), lambda m, j: (0, j)),
            pl.BlockSpec((D_MODEL, tf), lambda m, j: (0, j)),
            pl.BlockSpec((tf, D_MODEL), lambda m, j: (j, 0)),
            pl.BlockSpec((1, D_MODEL), lambda m, j: (0, 0)),
        ],
        out_specs=pl.BlockSpec((tm, D_MODEL), lambda m, j: (m, 0)),
        out_shape=jax.ShapeDtypeStruct((t, D_MODEL), F32),
        scratch_shapes=[pltpu.VMEM((tm, D_MODEL), BF16)],
        compiler_params=_params(("arbitrary", "arbitrary")),
        name="ffn",
    )(x2d, gain, wg_bf16, wu_bf16, wd_bf16, gain_final)


def kernel(x, norm_attn, w_in, lambda_qk, subln, w_out, norm_ffn, w_gate, w_up, w_down, norm_final):
    batch, seq, d_model = x.shape
    assert d_model == D_MODEL and w_in.shape == (1, D_MODEL, 6 * GROUP_WIDTH)
    assert w_gate.shape == (1, D_MODEL, D_FF) and seq % 256 == 0
    x2d = x.reshape(batch * seq, D_MODEL)
    tabs = _rope_tables(seq)

    proj_a, proj_b = _in_proj(x2d, norm_attn[0][None, :], w_in[0].astype(BF16), tabs, seq)
    ya = _attn_a(proj_a, batch, seq)
    yb, (wo_b, wg_b, wu_b, wd_b) = _attn_b(proj_b, lambda_qk[0], subln[0][:, None],
                                           (w_out[0], w_gate[0], w_up[0], w_down[0]), batch, seq)
    x1 = _out_proj(ya, yb, wo_b, x2d)
    out = _ffn(x1, norm_ffn[0][None, :], wg_b, wu_b, wd_b, norm_final[None, :])
    return out.reshape(batch, seq, D_MODEL)
```

```python
import functools
import math

import jax
import jax.numpy as jnp
import numpy as np
from jax import lax
from jax.experimental import pallas as pl
from jax.experimental.pallas import tpu as pltpu

D_MODEL = 2048
HEAD_DIM = 128
N_HEADS = 8
GROUP_WIDTH = N_HEADS * HEAD_DIM
DIFF_DIM = HEAD_DIM // 2
D_FF = 5632
ROPE_THETA = 500000.0
RMS_EPS = 1e-6
SUBLN_EPS = 1e-5
NEG_BIAS = -1e30
HALF_WINDOW = 64
DILATIONS = (1, 4, 16)
LAM_INIT = 0.8 - 0.6 * math.exp(-0.3 * 0)
LOG2E = math.log2(math.e)

VMEM_LIMIT_BYTES = 56 * 1024 * 1024
LANES = 128

BF16 = jnp.bfloat16
F32 = jnp.float32


def _params(semantics):
    return pltpu.CompilerParams(dimension_semantics=semantics, vmem_limit_bytes=VMEM_LIMIT_BYTES)


def _rms_scale(x, eps):
    return x * lax.rsqrt(jnp.mean(x * x, axis=-1, keepdims=True) + eps)


def _rope_tables(seq):
    lane = np.arange(LANES)
    kinds = ((HEAD_DIM, HEAD_DIM // 4, HEAD_DIM ** -0.5 * LOG2E), (HEAD_DIM, HEAD_DIM // 4, 1.0),
             (DIFF_DIM, DIFF_DIM // 4, DIFF_DIM ** -0.5 * LOG2E), (DIFF_DIM, DIFF_DIM // 4, 1.0))
    pos = jnp.arange(seq, dtype=F32)[:, None]
    cos, sin, in_lo, in_hi = [], [], [], []
    for period, rot_dim, _ in kinds:
        half = rot_dim // 2
        j = lane % period
        inv_freq = ROPE_THETA ** (-jnp.arange(0, rot_dim, 2, dtype=F32) / rot_dim)
        ang = pos * inv_freq[None, :]
        cos.append(jnp.take(jnp.cos(ang), j % half, axis=1))
        sin.append(jnp.take(jnp.sin(ang), j % half, axis=1))
        in_lo.append(j < half)
        in_hi.append((j >= half) & (j < rot_dim))
    in_lo, in_hi = np.stack(in_lo)[:, None, :], np.stack(in_hi)[:, None, :]
    scale = jnp.asarray([s for _, _, s in kinds], F32)[:, None, None]
    cos, sin = jnp.stack(cos), jnp.stack(sin)
    c = jnp.where(in_lo | in_hi, cos, 1.0) * scale
    s1 = jnp.where(in_lo, -sin, 0.0) * scale
    s2 = jnp.where(in_hi, sin, 0.0) * scale
    return jnp.stack([c, s1, s2], axis=1)


def _in_proj_kernel(x_ref, g_ref, w_ref, t_ref, oa_ref, ob_ref, h_ref, *, rows):
    n = pl.program_id(1)
    n_chunks = x_ref.shape[0] // rows

    def chunk(c):
        return pl.ds(c * rows, rows)

    def norm(c):
        h_ref[chunk(c), :] = (_rms_scale(x_ref[chunk(c), :], RMS_EPS) * g_ref[...]).astype(BF16)

    def matmul(c):
        return jnp.dot(h_ref[chunk(c), :], w_ref[...], preferred_element_type=F32)

    def rope_store(out_ref, shift):
        def store(c, acc):
            cos, s1, s2 = (t_ref[0, i, chunk(c), :] for i in range(3))
            for h in range(N_HEADS):
                cols = slice(h * HEAD_DIM, (h + 1) * HEAD_DIM)
                a = acc[:, cols]
                y = a * cos + pltpu.roll(a, HEAD_DIM - shift, 1) * s1 + pltpu.roll(a, shift, 1) * s2
                out_ref[h, chunk(c), :] = y.astype(out_ref.dtype)
        return store

    def plain_store(out_ref):
        def store(c, acc):
            for h in range(N_HEADS):
                out_ref[h, chunk(c), :] = acc[:, h * HEAD_DIM:(h + 1) * HEAD_DIM].astype(out_ref.dtype)
        return store

    def run(store, with_norm=False):
        accs = {}
        for i in range(n_chunks + 2):
            if with_norm and i < n_chunks:
                norm(i)
            if 1 <= i <= n_chunks:
                accs[i - 1] = matmul(i - 1)
            if i >= 2:
                store(i - 2, accs.pop(i - 2))

    pl.when(n == 0)(lambda: run(rope_store(oa_ref, HEAD_DIM // 8), with_norm=True))
    pl.when(n == 1)(lambda: run(rope_store(oa_ref, HEAD_DIM // 8)))
    pl.when(n == 2)(lambda: run(plain_store(oa_ref)))
    pl.when((n == 3) | (n == 4))(lambda: run(rope_store(ob_ref, DIFF_DIM // 8)))
    pl.when(n == 5)(lambda: run(plain_store(ob_ref)))


def _in_proj(x2d, gain, w_bf16, tabs, seq, tm=1024, rows=256):
    t = x2d.shape[0]
    seq_tiles = seq // tm
    tab_idx = lambda m, n: (jnp.where(n < 2, n, jnp.clip(n - 1, 2, 3)), 0, m % seq_tiles, 0)
    tab_spec = pl.BlockSpec((1, 3, tm, LANES), tab_idx)
    return pl.pallas_call(
        functools.partial(_in_proj_kernel, rows=rows),
        grid=(t // tm, 6),
        in_specs=[
            pl.BlockSpec((tm, D_MODEL), lambda m, n: (m, 0)),
            pl.BlockSpec((1, D_MODEL), lambda m, n: (0, 0)),
            pl.BlockSpec((D_MODEL, GROUP_WIDTH), lambda m, n: (0, n)),
            tab_spec,
        ],
        out_specs=[
            pl.BlockSpec((N_HEADS, tm, HEAD_DIM), lambda m, n: (jnp.minimum(n, 2), m, 0)),
            pl.BlockSpec((N_HEADS, tm, HEAD_DIM), lambda m, n: (jnp.maximum(n - 3, 0), m, 0)),
        ],
        out_shape=[
            jax.ShapeDtypeStruct((3 * N_HEADS, t, HEAD_DIM), F32),
            jax.ShapeDtypeStruct((3 * N_HEADS, t, HEAD_DIM), BF16),
        ],
        scratch_shapes=[pltpu.VMEM((tm, D_MODEL), BF16)],
        compiler_params=_params(("arbitrary", "arbitrary")),
        name="in_proj",
    )(x2d, gain, w_bf16, tabs)


TQ_A = 128
TQ_B = 256
ONES_ROWS = 16
CAST_ROWS = 16


def _window_bias(n_q, n_k, offset):
    row = lax.broadcasted_iota(jnp.int32, (n_q, n_k), 0)
    col = lax.broadcasted_iota(jnp.int32, (n_q, n_k), 1)
    dist = col + offset - row
    return jnp.where(jnp.abs(dist) <= HALF_WINDOW, 0.0, NEG_BIAS).astype(F32)


def _attn_a_steps(q_ref, k_ref, v_ref, o_ref, st_ref, qs_ref, ks_ref, vs_ref, ob_ref, lb_ref, *, seq):
    tq = TQ_A
    n_dil = len(DILATIONS)
    assert DILATIONS == (1, 4, 16)
    len4, len16 = seq // 4, seq // 16

    for a, (src, dst) in enumerate(((q_ref, qs_ref), (k_ref, ks_ref), (v_ref, vs_ref))):
        dst[0, :, :HEAD_DIM] = src[...].astype(BF16)
        for r4 in range(4):
            part = src[pl.ds(r4, len4, stride=4), :]
            st_ref[a, pl.ds(r4 * len4, len4), :] = part
            dst[1, pl.ds(r4 * len4, len4), :HEAD_DIM] = part.astype(BF16)
        for r4 in range(4):
            for j in range(4):
                part = st_ref[a, pl.ds(r4 * len4 + j, len16, stride=4), :]
                dst[2, pl.ds((r4 + 4 * j) * len16, len16), :HEAD_DIM] = part.astype(BF16)
    vs_ref[:, :, HEAD_DIM:] = jnp.ones((n_dil, seq, HEAD_DIM), BF16)
    yield

    tiles = []
    for g, dil in list(enumerate(DILATIONS))[1:] + list(enumerate(DILATIONS))[:1]:
        length = seq // dil
        kw = min(256, length)
        for r in range(dil):
            for tile in range(length // tq):
                q0 = tile * tq
                ws = min(max(q0 - HALF_WINDOW, 0), length - kw)
                tiles.append((g, dil, r, r * length, q0, ws, kw))

    def scores(t):
        g, dil, r, base, q0, ws, kw = t
        qt = qs_ref[g, pl.ds(base + q0, tq), :]
        kt = ks_ref[g, pl.ds(base + ws, kw), :]
        s = lax.dot_general(qt, kt, (((1,), (1,)), ((), ())), preferred_element_type=F32)
        return s + _window_bias(tq, kw, ws - q0)

    def softmax(s):
        m = jnp.max(s, axis=-1, keepdims=True)
        return jnp.exp2(s - m).astype(BF16), m

    def finish(t, p, m):
        g, dil, r, base, q0, ws, kw = t
        ov = jnp.dot(p, vs_ref[g, pl.ds(base + ws, kw), :], preferred_element_type=F32)
        den = ov[:, HEAD_DIM:]
        o0, l0 = ov[:, :HEAD_DIM] / den, m + jnp.log2(den)
        if dil > 1:
            out_rows = pl.ds(q0 * 4 + r, tq, stride=4) if dil == 4 else pl.ds(q0 * 17 + r, tq, stride=17)
            ob_ref[g - 1, out_rows, :] = o0
            lb_ref[g - 1, out_rows, :] = l0
            return
        rows = pl.ds(q0, tq)

        def skewed(ref):
            return jnp.concatenate([ref[1, pl.ds(q0 + q0 // 16 + 17 * k, 16), :] for k in range(tq // 16)],
                                   axis=0)

        l1, l2 = lb_ref[0, rows, :], skewed(lb_ref)
        mx = jnp.maximum(jnp.maximum(l0, l1), l2)
        e0, e1, e2 = jnp.exp2(l0 - mx), jnp.exp2(l1 - mx), jnp.exp2(l2 - mx)
        merged = (e0 * o0 + e1 * ob_ref[0, rows, :] + e2 * skewed(ob_ref)) / (e0 + e1 + e2)
        o_ref[rows, :] = merged.astype(BF16)

    s_vals, p_vals = {}, {}
    for i in range(len(tiles) + 2):
        if i < len(tiles):
            s_vals[i] = scores(tiles[i])
        if 1 <= i <= len(tiles):
            p_vals[i - 1] = softmax(s_vals.pop(i - 1))
        if i >= 2:
            finish(tiles[i - 2], *p_vals.pop(i - 2))
        yield


def _attn_b_steps(q_ref, k_ref, v_ref, lq_ref, g_ref, w_f32_refs, o_ref, w_bf16_refs, vt_ref):
    seq = q_ref.shape[0]
    tq = TQ_B
    vt_ref[:HEAD_DIM, :] = v_ref[...].astype(F32).T.astype(BF16)
    vt_ref[HEAD_DIM:, :] = jnp.ones((ONES_ROWS, seq), BF16)

    cast_jobs = [(src, dst, r0) for src, dst in zip(w_f32_refs, w_bf16_refs)
                 for r0 in range(0, src.shape[0], CAST_ROWS)]

    lq = lq_ref[...]
    lam = (jnp.exp(jnp.sum(lq[0:1] * lq[1:2], axis=-1, keepdims=True))
           - jnp.exp(jnp.sum(lq[2:3] * lq[3:4], axis=-1, keepdims=True)) + LAM_INIT)

    lane = lax.broadcasted_iota(jnp.int32, (tq, HEAD_DIM), 1)
    zero = jnp.zeros((tq, HEAD_DIM), BF16)
    yield

    def scores_t(qm):
        st = lax.dot_general(k_ref[...], qm, (((1,), (1,)), ((), ())),
                             preferred_element_type=F32)
        sb = st.astype(BF16)
        return sb, jnp.max(sb, axis=0, keepdims=True)

    def probs_t(staged):
        sb, m = staged
        return jnp.exp2((sb - m).astype(F32)).astype(BF16)

    def weighted_v(pt):
        ov = jnp.dot(vt_ref[...], pt, preferred_element_type=F32)
        return ov[:HEAD_DIM] / ov[HEAD_DIM:HEAD_DIM + 1]

    n_tiles = seq // tq
    n_iters = n_tiles + 2
    scores, probs = {}, {}
    for i in range(n_iters):
        for src, dst, r0 in cast_jobs[i::n_iters]:
            dst[pl.ds(r0, CAST_ROWS), :] = src[pl.ds(r0, CAST_ROWS), :].astype(BF16)
        if i < n_tiles:
            q = q_ref[i * tq:(i + 1) * tq, :]
            scores[i] = (scores_t(jnp.where(lane < DIFF_DIM, q, zero)),
                         scores_t(jnp.where(lane >= DIFF_DIM, q, zero)))
        if 1 <= i <= n_tiles:
            st1, st2 = scores.pop(i - 1)
            probs[i - 1] = (probs_t(st1), probs_t(st2))
        if i >= 2:
            pt1, pt2 = probs.pop(i - 2)
            yt = weighted_v(pt1) - lam * weighted_v(pt2)
            yt = yt * lax.rsqrt(jnp.mean(yt * yt, axis=0, keepdims=True) + SUBLN_EPS)
            yt = yt * g_ref[...] * (1.0 - LAM_INIT)
            o_ref[(i - 2) * tq:(i - 1) * tq, :] = yt.T.astype(BF16)
        yield


def _head_block(seq, group):
    return pl.BlockSpec((None, seq, HEAD_DIM), lambda b, h: (group * N_HEADS + h, b, 0))


def _attn_a_kernel(*refs, seq, heads):
    in_refs, o_ref, scratch = refs[:3 * heads], refs[3 * heads], refs[3 * heads + 1:]
    for hd in range(heads):
        own = [ref.at[hd] for ref in scratch]
        for _ in _attn_a_steps(*in_refs[3 * hd:3 * hd + 3], o_ref.at[hd], *own, seq=seq):
            pass


def _attn_a(proj_a, batch, seq, heads=2):
    n_dil = len(DILATIONS)
    in_specs = [pl.BlockSpec((None, seq, HEAD_DIM),
                             lambda b, h, g=g, hd=hd: (g * N_HEADS + h * heads + hd, b, 0))
                for hd in range(heads) for g in range(3)]
    return pl.pallas_call(
        functools.partial(_attn_a_kernel, seq=seq, heads=heads),
        grid=(batch, N_HEADS // heads),
        in_specs=in_specs,
        out_specs=pl.BlockSpec((heads, seq, HEAD_DIM), lambda b, h: (h, b, 0)),
        out_shape=jax.ShapeDtypeStruct((N_HEADS, batch * seq, HEAD_DIM), BF16),
        scratch_shapes=[
            pltpu.VMEM((heads, 3, seq, HEAD_DIM), F32),
            pltpu.VMEM((heads, n_dil, seq, HEAD_DIM), BF16),
            pltpu.VMEM((heads, n_dil, seq, HEAD_DIM), BF16),
            pltpu.VMEM((heads, n_dil, seq, 2 * HEAD_DIM), BF16),
            pltpu.VMEM((heads, n_dil - 1, seq + seq // 16, HEAD_DIM), F32),
            pltpu.VMEM((heads, n_dil - 1, seq + seq // 16, LANES), F32),
        ],
        compiler_params=_params(("arbitrary", "arbitrary")),
        name="attn_a",
    )(*([proj_a] * (3 * heads)))


def _attn_b_kernel(*refs, n_weights):
    q_ref, k_ref, v_ref, lq_ref, g_ref = refs[:5]
    w_f32_refs = refs[5:5 + n_weights]
    o_ref = refs[5 + n_weights]
    w_bf16_refs = refs[6 + n_weights:6 + 2 * n_weights]
    vt_ref = refs[6 + 2 * n_weights]
    for _ in _attn_b_steps(q_ref, k_ref, v_ref, lq_ref, g_ref, w_f32_refs, o_ref, w_bf16_refs, vt_ref):
        pass


def _attn_b(proj_b, lambda_qk, subln, weights, batch, seq):
    steps = batch * N_HEADS
    w_specs = []
    for w in weights:
        rows = w.shape[0] // steps
        assert rows * steps == w.shape[0] and rows % CAST_ROWS == 0
        w_specs.append(pl.BlockSpec((rows, w.shape[1]), lambda b, h: (b * N_HEADS + h, 0)))
    outs = pl.pallas_call(
        functools.partial(_attn_b_kernel, n_weights=len(weights)),
        grid=(batch, N_HEADS),
        in_specs=[_head_block(seq, 0), _head_block(seq, 1), _head_block(seq, 2),
                  pl.BlockSpec((4, DIFF_DIM), lambda b, h: (0, 0)),
                  pl.BlockSpec((HEAD_DIM, 1), lambda b, h: (0, 0))] + w_specs,
        out_specs=[_head_block(seq, 0)] + w_specs,
        out_shape=[jax.ShapeDtypeStruct((N_HEADS, batch * seq, HEAD_DIM), BF16)]
        + [jax.ShapeDtypeStruct(w.shape, BF16) for w in weights],
        scratch_shapes=[pltpu.VMEM((HEAD_DIM + ONES_ROWS, seq), BF16)],
        compiler_params=_params(("arbitrary", "arbitrary")),
        name="attn_b",
    )(proj_b, proj_b, proj_b, lambda_qk, subln, *weights)
    return outs[0], outs[1:]


def _out_proj_kernel(ya_ref, yb_ref, w_ref, x_ref, o_ref, *, rows):
    for c in range(x_ref.shape[0] // rows):
        sl = pl.ds(c * rows, rows)
        y = jnp.concatenate([ref[h, sl, :] for ref in (ya_ref, yb_ref) for h in range(N_HEADS)], axis=1)
        o_ref[sl, :] = x_ref[sl, :] + jnp.dot(y, w_ref[...], preferred_element_type=F32)


def _out_proj(ya, yb, w_bf16, x2d, tm=512, rows=256):
    t = x2d.shape[0]
    return pl.pallas_call(
        functools.partial(_out_proj_kernel, rows=rows),
        grid=(t // tm,),
        in_specs=[
            pl.BlockSpec((N_HEADS, tm, HEAD_DIM), lambda m: (0, m, 0)),
            pl.BlockSpec((N_HEADS, tm, HEAD_DIM), lambda m: (0, m, 0)),
            pl.BlockSpec((2 * GROUP_WIDTH, D_MODEL), lambda m: (0, 0)),
            pl.BlockSpec((tm, D_MODEL), lambda m: (m, 0)),
        ],
        out_specs=pl.BlockSpec((tm, D_MODEL), lambda m: (m, 0)),
        out_shape=jax.ShapeDtypeStruct((t, D_MODEL), F32),
        compiler_params=_params(("arbitrary",)),
        name="out_proj",
    )(ya, yb, w_bf16, x2d)


def _ffn_kernel(x_ref, gn_ref, wg_ref, wu_ref, wd_ref, gf_ref, o_ref, h_ref, *, rows):
    j = pl.program_id(1)
    last = pl.num_programs(1) - 1
    n_chunks = x_ref.shape[0] // rows

    def chunk(c):
        return pl.ds(c * rows, rows)

    def norm(c):
        h_ref[chunk(c), :] = (_rms_scale(x_ref[chunk(c), :], RMS_EPS) * gn_ref[...]).astype(BF16)

    def run(first, final):
        acts = {}
        for i in range(n_chunks + 2):
            if first and i < n_chunks:
                norm(i)
            if 1 <= i <= n_chunks:
                h = h_ref[chunk(i - 1), :]
                gate = jnp.dot(h, wg_ref[...], preferred_element_type=F32)
                up = jnp.dot(h, wu_ref[...], preferred_element_type=F32)
                acts[i - 1] = (gate / (1.0 + jnp.exp(-gate)) * up).astype(BF16)
            if i >= 2:
                dst = chunk(i - 2)
                base = x_ref[dst, :] if first else o_ref[dst, :]
                acc = base + jnp.dot(acts.pop(i - 2), wd_ref[...], preferred_element_type=F32)
                o_ref[dst, :] = _rms_scale(acc, RMS_EPS) * gf_ref[...] if final else acc

    pl.when(j == 0)(lambda: run(True, False))
    pl.when((j > 0) & (j < last))(lambda: run(False, False))
    pl.when(j == last)(lambda: run(False, True))


def _ffn(x2d, gain, wg_bf16, wu_bf16, wd_bf16, gain_final, tm=1024, tf=512, rows=256):
    t = x2d.shape[0]
    assert D_FF // tf >= 2
    return pl.pallas_call(
        functools.partial(_ffn_kernel, rows=rows),
        grid=(t // tm, D_FF // tf),
        in_specs=[
            pl.BlockSpec((tm, D_MODEL), lambda m, j: (m, 0)),
            pl.BlockSpec((1, D_MODEL), lambda m, j: (0, 0)),
            pl.BlockSpec((D_MODEL, tf), lambda m, j: (0, j)),
            pl.BlockSpec((D_MODEL, tf), lambda m, j: (0, j)),
            pl.BlockSpec((tf, D_MODEL), lambda m, j: (j, 0)),
            pl.BlockSpec((1, D_MODEL), lambda m, j: (0, 0)),
        ],
        out_specs=pl.BlockSpec((tm, D_MODEL), lambda m, j: (m, 0)),
        out_shape=jax.ShapeDtypeStruct((t, D_MODEL), F32),
        scratch_shapes=[pltpu.VMEM((tm, D_MODEL), BF16)],
        compiler_params=_params(("arbitrary", "arbitrary")),
        name="ffn",
    )(x2d, gain, wg_bf16, wu_bf16, wd_bf16, gain_final)


def kernel(x, norm_attn, w_in, lambda_qk, subln, w_out, norm_ffn, w_gate, w_up, w_down, norm_final):
    batch, seq, d_model = x.shape
    assert d_model == D_MODEL and w_in.shape == (1, D_MODEL, 6 * GROUP_WIDTH)
    assert w_gate.shape == (1, D_MODEL, D_FF) and seq % 256 == 0
    x2d = x.reshape(batch * seq, D_MODEL)
    tabs = _rope_tables(seq)

    proj_a, proj_b = _in_proj(x2d, norm_attn[0][None, :], w_in[0].astype(BF16), tabs, seq)
    ya = _attn_a(proj_a, batch, seq)
    yb, (wo_b, wg_b, wu_b, wd_b) = _attn_b(proj_b, lambda_qk[0], subln[0][:, None],
                                           (w_out[0], w_gate[0], w_up[0], w_down[0]), batch, seq)
    x1 = _out_proj(ya, yb, wo_b, x2d)
    out = _ffn(x1, norm_ffn[0][None, :], wg_b, wu_b, wd_b, norm_final[None, :])
    return out.reshape(batch, seq, D_MODEL)
```

```python
import functools
import math

import jax
import jax.numpy as jnp
import numpy as np
from jax import lax
from jax.experimental import pallas as pl
from jax.experimental.pallas import tpu as pltpu

D_MODEL = 2048
HEAD_DIM = 128
N_HEADS = 8
GROUP_WIDTH = N_HEADS * HEAD_DIM
DIFF_DIM = HEAD_DIM // 2
D_FF = 5632
ROPE_THETA = 500000.0
RMS_EPS = 1e-6
SUBLN_EPS = 1e-5
NEG_BIAS = -1e30
HALF_WINDOW = 64
DILATIONS = (1, 4, 16)
LAM_INIT = 0.8 - 0.6 * math.exp(-0.3 * 0)
LOG2E = math.log2(math.e)

VMEM_LIMIT_BYTES = 56 * 1024 * 1024
LANES = 128

BF16 = jnp.bfloat16
F32 = jnp.float32


def _params(semantics):
    return pltpu.CompilerParams(dimension_semantics=semantics, vmem_limit_bytes=VMEM_LIMIT_BYTES)


def _rms_scale(x, eps):
    return x * lax.rsqrt(jnp.mean(x * x, axis=-1, keepdims=True) + eps)


def _rope_tables(seq):
    lane = np.arange(LANES)
    kinds = ((HEAD_DIM, HEAD_DIM // 4, HEAD_DIM ** -0.5 * LOG2E), (HEAD_DIM, HEAD_DIM // 4, 1.0),
             (DIFF_DIM, DIFF_DIM // 4, DIFF_DIM ** -0.5 * LOG2E), (DIFF_DIM, DIFF_DIM // 4, 1.0))
    pos = jnp.arange(seq, dtype=F32)[:, None]
    cos, sin, in_lo, in_hi = [], [], [], []
    for period, rot_dim, _ in kinds:
        half = rot_dim // 2
        j = lane % period
        inv_freq = ROPE_THETA ** (-jnp.arange(0, rot_dim, 2, dtype=F32) / rot_dim)
        ang = pos * inv_freq[None, :]
        cos.append(jnp.take(jnp.cos(ang), j % half, axis=1))
        sin.append(jnp.take(jnp.sin(ang), j % half, axis=1))
        in_lo.append(j < half)
        in_hi.append((j >= half) & (j < rot_dim))
    in_lo, in_hi = np.stack(in_lo)[:, None, :], np.stack(in_hi)[:, None, :]
    scale = jnp.asarray([s for _, _, s in kinds], F32)[:, None, None]
    cos, sin = jnp.stack(cos), jnp.stack(sin)
    c = jnp.where(in_lo | in_hi, cos, 1.0) * scale
    s1 = jnp.where(in_lo, -sin, 0.0) * scale
    s2 = jnp.where(in_hi, sin, 0.0) * scale
    return jnp.stack([c, s1, s2], axis=1)


W_SLOTS = 3


def _in_proj_kernel(x_ref, g_ref, w_hbm, t_ref, oa_ref, ob_ref, h_ref, w_scr, w_sem, *, rows):
    n = pl.program_id(1)
    n_groups = pl.num_programs(1)
    step = pl.program_id(0) * n_groups + n
    total = pl.num_programs(0) * n_groups
    n_chunks = x_ref.shape[0] // rows

    def weight_copy(s):
        col0 = pl.multiple_of((s % n_groups) * GROUP_WIDTH, GROUP_WIDTH)
        slot = s % W_SLOTS
        return pltpu.make_async_copy(w_hbm.at[:, pl.ds(col0, GROUP_WIDTH)], w_scr.at[slot], w_sem.at[slot])

    @pl.when(step == 0)
    def _():
        weight_copy(step).start()
        weight_copy(step + 1).start()

    @pl.when(step + 2 < total)
    def _():
        weight_copy(step + 2).start()

    weight_copy(step).wait()
    w_ref = w_scr.at[step % W_SLOTS]

    def chunk(c):
        return pl.ds(c * rows, rows)

    def norm(c):
        h_ref[chunk(c), :] = (_rms_scale(x_ref[chunk(c), :], RMS_EPS) * g_ref[...]).astype(BF16)

    def matmul(c):
        return jnp.dot(h_ref[chunk(c), :], w_ref[...], preferred_element_type=F32)

    def rope_store(out_ref, shift):
        def store(c, acc):
            cos, s1, s2 = (t_ref[0, i, chunk(c), :] for i in range(3))
            for h in range(N_HEADS):
                cols = slice(h * HEAD_DIM, (h + 1) * HEAD_DIM)
                a = acc[:, cols]
                y = a * cos + pltpu.roll(a, HEAD_DIM - shift, 1) * s1 + pltpu.roll(a, shift, 1) * s2
                out_ref[h, chunk(c), :] = y.astype(out_ref.dtype)
        return store

    def plain_store(out_ref):
        def store(c, acc):
            for h in range(N_HEADS):
                out_ref[h, chunk(c), :] = acc[:, h * HEAD_DIM:(h + 1) * HEAD_DIM].astype(out_ref.dtype)
        return store

    def run(store, with_norm=False):
        accs = {}
        for i in range(n_chunks + 2):
            if with_norm and i < n_chunks:
                norm(i)
            if 1 <= i <= n_chunks:
                accs[i - 1] = matmul(i - 1)
            if i >= 2:
                store(i - 2, accs.pop(i - 2))

    pl.when(n == 0)(lambda: run(rope_store(oa_ref, HEAD_DIM // 8), with_norm=True))
    pl.when(n == 1)(lambda: run(rope_store(oa_ref, HEAD_DIM // 8)))
    pl.when(n == 2)(lambda: run(plain_store(oa_ref)))
    pl.when((n == 3) | (n == 4))(lambda: run(rope_store(ob_ref, DIFF_DIM // 8)))
    pl.when(n == 5)(lambda: run(plain_store(ob_ref)))


def _in_proj(x2d, gain, w_bf16, tabs, seq, tm=1024, rows=256):
    t = x2d.shape[0]
    seq_tiles = seq // tm
    tab_idx = lambda m, n: (jnp.where(n < 2, n, jnp.clip(n - 1, 2, 3)), 0, m % seq_tiles, 0)
    tab_spec = pl.BlockSpec((1, 3, tm, LANES), tab_idx)
    return pl.pallas_call(
        functools.partial(_in_proj_kernel, rows=rows),
        grid=(t // tm, 6),
        in_specs=[
            pl.BlockSpec((tm, D_MODEL), lambda m, n: (m, 0)),
            pl.BlockSpec((1, D_MODEL), lambda m, n: (0, 0)),
            pl.BlockSpec(memory_space=pl.ANY),
            tab_spec,
        ],
        out_specs=[
            pl.BlockSpec((N_HEADS, tm, HEAD_DIM), lambda m, n: (jnp.minimum(n, 2), m, 0)),
            pl.BlockSpec((N_HEADS, tm, HEAD_DIM), lambda m, n: (jnp.maximum(n - 3, 0), m, 0)),
        ],
        out_shape=[
            jax.ShapeDtypeStruct((3 * N_HEADS, t, HEAD_DIM), F32),
            jax.ShapeDtypeStruct((3 * N_HEADS, t, HEAD_DIM), BF16),
        ],
        scratch_shapes=[
            pltpu.VMEM((tm, D_MODEL), BF16),
            pltpu.VMEM((W_SLOTS, D_MODEL, GROUP_WIDTH), BF16),
            pltpu.SemaphoreType.DMA((W_SLOTS,)),
        ],
        compiler_params=_params(("arbitrary", "arbitrary")),
        name="in_proj",
    )(x2d, gain, w_bf16, tabs)


TQ_A = 128
TQ_B = 256
ONES_ROWS = 16
CAST_ROWS = 16


def _window_bias(n_q, n_k, offset):
    row = lax.broadcasted_iota(jnp.int32, (n_q, n_k), 0)
    col = lax.broadcasted_iota(jnp.int32, (n_q, n_k), 1)
    dist = col + offset - row
    return jnp.where(jnp.abs(dist) <= HALF_WINDOW, 0.0, NEG_BIAS).astype(F32)


def _attn_a_steps(q_ref, k_ref, v_ref, o_ref, st_ref, qs_ref, ks_ref, vs_ref, ob_ref, lb_ref, *, seq):
    tq = TQ_A
    n_dil = len(DILATIONS)
    assert DILATIONS == (1, 4, 16)
    len4, len16 = seq // 4, seq // 16

    for a, (src, dst) in enumerate(((q_ref, qs_ref), (k_ref, ks_ref), (v_ref, vs_ref))):
        dst[0, :, :HEAD_DIM] = src[...].astype(BF16)
        for r4 in range(4):
            part = src[pl.ds(r4, len4, stride=4), :]
            st_ref[a, pl.ds(r4 * len4, len4), :] = part
            dst[1, pl.ds(r4 * len4, len4), :HEAD_DIM] = part.astype(BF16)
        for r4 in range(4):
            for j in range(4):
                part = st_ref[a, pl.ds(r4 * len4 + j, len16, stride=4), :]
                dst[2, pl.ds((r4 + 4 * j) * len16, len16), :HEAD_DIM] = part.astype(BF16)
    vs_ref[:, :, HEAD_DIM:] = jnp.ones((n_dil, seq, HEAD_DIM), BF16)
    yield

    tiles = []
    for g, dil in list(enumerate(DILATIONS))[1:] + list(enumerate(DILATIONS))[:1]:
        length = seq // dil
        kw = min(256, length)
        for r in range(dil):
            for tile in range(length // tq):
                q0 = tile * tq
                ws = min(max(q0 - HALF_WINDOW, 0), length - kw)
                tiles.append((g, dil, r, r * length, q0, ws, kw))

    def scores(t):
        g, dil, r, base, q0, ws, kw = t
        qt = qs_ref[g, pl.ds(base + q0, tq), :]
        kt = ks_ref[g, pl.ds(base + ws, kw), :]
        s = lax.dot_general(qt, kt, (((1,), (1,)), ((), ())), preferred_element_type=F32)
        return s + _window_bias(tq, kw, ws - q0)

    def softmax(s):
        m = jnp.max(s, axis=-1, keepdims=True)
        return jnp.exp2(s - m).astype(BF16), m

    def finish(t, p, m):
        g, dil, r, base, q0, ws, kw = t
        ov = jnp.dot(p, vs_ref[g, pl.ds(base + ws, kw), :], preferred_element_type=F32)
        den = ov[:, HEAD_DIM:]
        o0, l0 = ov[:, :HEAD_DIM] / den, m + jnp.log2(den)
        if dil > 1:
            out_rows = pl.ds(q0 * 4 + r, tq, stride=4) if dil == 4 else pl.ds(q0 * 17 + r, tq, stride=17)
            ob_ref[g - 1, out_rows, :] = o0
            lb_ref[g - 1, out_rows, :] = l0
            return
        rows = pl.ds(q0, tq)

        def skewed(ref):
            return jnp.concatenate([ref[1, pl.ds(q0 + q0 // 16 + 17 * k, 16), :] for k in range(tq // 16)],
                                   axis=0)

        l1, l2 = lb_ref[0, rows, :], skewed(lb_ref)
        mx = jnp.maximum(jnp.maximum(l0, l1), l2)
        e0, e1, e2 = jnp.exp2(l0 - mx), jnp.exp2(l1 - mx), jnp.exp2(l2 - mx)
        merged = (e0 * o0 + e1 * ob_ref[0, rows, :] + e2 * skewed(ob_ref)) / (e0 + e1 + e2)
        o_ref[rows, :] = merged.astype(BF16)

    s_vals, p_vals = {}, {}
    for i in range(len(tiles) + 2):
        if i < len(tiles):
            s_vals[i] = scores(tiles[i])
        if 1 <= i <= len(tiles):
            p_vals[i - 1] = softmax(s_vals.pop(i - 1))
        if i >= 2:
            finish(tiles[i - 2], *p_vals.pop(i - 2))
        yield


def _attn_b_steps(q_ref, k_ref, v_ref, lq_ref, g_ref, w_f32_refs, o_ref, w_bf16_refs, vt_ref):
    seq = q_ref.shape[0]
    tq = TQ_B
    vt_ref[:HEAD_DIM, :] = v_ref[...].astype(F32).T.astype(BF16)
    vt_ref[HEAD_DIM:, :] = jnp.ones((ONES_ROWS, seq), BF16)

    cast_jobs = [(src, dst, r0) for src, dst in zip(w_f32_refs, w_bf16_refs)
                 for r0 in range(0, src.shape[0], CAST_ROWS)]

    lq = lq_ref[...]
    lam = (jnp.exp(jnp.sum(lq[0:1] * lq[1:2], axis=-1, keepdims=True))
           - jnp.exp(jnp.sum(lq[2:3] * lq[3:4], axis=-1, keepdims=True)) + LAM_INIT)

    lane = lax.broadcasted_iota(jnp.int32, (tq, HEAD_DIM), 1)
    zero = jnp.zeros((tq, HEAD_DIM), BF16)
    yield

    def scores_t(qm):
        st = lax.dot_general(k_ref[...], qm, (((1,), (1,)), ((), ())),
                             preferred_element_type=F32)
        return st.astype(BF16), jnp.max(st, axis=0, keepdims=True)

    def probs_t(staged):
        sb, m = staged
        return jnp.exp2(sb.astype(F32) - m).astype(BF16)

    def weighted_v(pt):
        ov = jnp.dot(vt_ref[...], pt, preferred_element_type=F32)
        return ov[:HEAD_DIM] / ov[HEAD_DIM:HEAD_DIM + 1]

    n_tiles = seq // tq
    n_iters = n_tiles + 2
    scores, probs = {}, {}
    for i in range(n_iters):
        for src, dst, r0 in cast_jobs[i::n_iters]:
            dst[pl.ds(r0, CAST_ROWS), :] = src[pl.ds(r0, CAST_ROWS), :].astype(BF16)
        if i < n_tiles:
            q = q_ref[i * tq:(i + 1) * tq, :]
            scores[i] = (scores_t(jnp.where(lane < DIFF_DIM, q, zero)),
                         scores_t(jnp.where(lane >= DIFF_DIM, q, zero)))
        if 1 <= i <= n_tiles:
            st1, st2 = scores.pop(i - 1)
            probs[i - 1] = (probs_t(st1), probs_t(st2))
        if i >= 2:
            pt1, pt2 = probs.pop(i - 2)
            yt = weighted_v(pt1) - lam * weighted_v(pt2)
            yt = yt * lax.rsqrt(jnp.mean(yt * yt, axis=0, keepdims=True) + SUBLN_EPS)
            yt = yt * g_ref[...] * (1.0 - LAM_INIT)
            o_ref[(i - 2) * tq:(i - 1) * tq, :] = yt.T.astype(BF16)
        yield


def _head_block(seq, group):
    return pl.BlockSpec((None, seq, HEAD_DIM), lambda b, h: (group * N_HEADS + h, b, 0))


def _attn_a_kernel(*refs, seq, heads):
    in_refs, o_ref, scratch = refs[:3 * heads], refs[3 * heads], refs[3 * heads + 1:]
    for hd in range(heads):
        own = [ref.at[hd] for ref in scratch]
        for _ in _attn_a_steps(*in_refs[3 * hd:3 * hd + 3], o_ref.at[hd], *own, seq=seq):
            pass


def _attn_a(proj_a, batch, seq, heads=2):
    n_dil = len(DILATIONS)
    in_specs = [pl.BlockSpec((None, seq, HEAD_DIM),
                             lambda b, h, g=g, hd=hd: (g * N_HEADS + h * heads + hd, b, 0))
                for hd in range(heads) for g in range(3)]
    return pl.pallas_call(
        functools.partial(_attn_a_kernel, seq=seq, heads=heads),
        grid=(batch, N_HEADS // heads),
        in_specs=in_specs,
        out_specs=pl.BlockSpec((heads, seq, HEAD_DIM), lambda b, h: (h, b, 0)),
        out_shape=jax.ShapeDtypeStruct((N_HEADS, batch * seq, HEAD_DIM), BF16),
        scratch_shapes=[
            pltpu.VMEM((heads, 3, seq, HEAD_DIM), F32),
            pltpu.VMEM((heads, n_dil, seq, HEAD_DIM), BF16),
            pltpu.VMEM((heads, n_dil, seq, HEAD_DIM), BF16),
            pltpu.VMEM((heads, n_dil, seq, 2 * HEAD_DIM), BF16),
            pltpu.VMEM((heads, n_dil - 1, seq + seq // 16, HEAD_DIM), F32),
            pltpu.VMEM((heads, n_dil - 1, seq + seq // 16, LANES), F32),
        ],
        compiler_params=_params(("arbitrary", "arbitrary")),
        name="attn_a",
    )(*([proj_a] * (3 * heads)))


def _attn_b_kernel(*refs, n_weights):
    q_ref, k_ref, v_ref, lq_ref, g_ref = refs[:5]
    w_f32_refs = refs[5:5 + n_weights]
    o_ref = refs[5 + n_weights]
    w_bf16_refs = refs[6 + n_weights:6 + 2 * n_weights]
    vt_ref = refs[6 + 2 * n_weights]
    for _ in _attn_b_steps(q_ref, k_ref, v_ref, lq_ref, g_ref, w_f32_refs, o_ref, w_bf16_refs, vt_ref):
        pass


def _attn_b(proj_b, lambda_qk, subln, weights, batch, seq):
    steps = batch * N_HEADS
    w_specs = []
    for w in weights:
        rows = w.shape[0] // steps
        assert rows * steps == w.shape[0] and rows % CAST_ROWS == 0
        w_specs.append(pl.BlockSpec((rows, w.shape[1]), lambda b, h: (b * N_HEADS + h, 0)))
    outs = pl.pallas_call(
        functools.partial(_attn_b_kernel, n_weights=len(weights)),
        grid=(batch, N_HEADS),
        in_specs=[_head_block(seq, 0), _head_block(seq, 1), _head_block(seq, 2),
                  pl.BlockSpec((4, DIFF_DIM), lambda b, h: (0, 0)),
                  pl.BlockSpec((HEAD_DIM, 1), lambda b, h: (0, 0))] + w_specs,
        out_specs=[_head_block(seq, 0)] + w_specs,
        out_shape=[jax.ShapeDtypeStruct((N_HEADS, batch * seq, HEAD_DIM), BF16)]
        + [jax.ShapeDtypeStruct(w.shape, BF16) for w in weights],
        scratch_shapes=[pltpu.VMEM((HEAD_DIM + ONES_ROWS, seq), BF16)],
        compiler_params=_params(("arbitrary", "arbitrary")),
        name="attn_b",
    )(proj_b, proj_b, proj_b, lambda_qk, subln, *weights)
    return outs[0], outs[1:]


def _out_proj_kernel(ya_ref, yb_ref, w_ref, x_ref, o_ref, *, rows):
    for c in range(x_ref.shape[0] // rows):
        sl = pl.ds(c * rows, rows)
        y = jnp.concatenate([ref[h, sl, :] for ref in (ya_ref, yb_ref) for h in range(N_HEADS)], axis=1)
        o_ref[sl, :] = x_ref[sl, :] + jnp.dot(y, w_ref[...], preferred_element_type=F32)


def _out_proj(ya, yb, w_bf16, x2d, tm=512, rows=256):
    t = x2d.shape[0]
    return pl.pallas_call(
        functools.partial(_out_proj_kernel, rows=rows),
        grid=(t // tm,),
        in_specs=[
            pl.BlockSpec((N_HEADS, tm, HEAD_DIM), lambda m: (0, m, 0)),
            pl.BlockSpec((N_HEADS, tm, HEAD_DIM), lambda m: (0, m, 0)),
            pl.BlockSpec((2 * GROUP_WIDTH, D_MODEL), lambda m: (0, 0)),
            pl.BlockSpec((tm, D_MODEL), lambda m: (m, 0)),
        ],
        out_specs=pl.BlockSpec((tm, D_MODEL), lambda m: (m, 0)),
        out_shape=jax.ShapeDtypeStruct((t, D_MODEL), F32),
        compiler_params=_params(("arbitrary",)),
        name="out_proj",
    )(ya, yb, w_bf16, x2d)


def _ffn_kernel(x_ref, gn_ref, wg_ref, wu_ref, wd_ref, gf_ref, o_ref, h_ref, *, rows):
    j = pl.program_id(1)
    last = pl.num_programs(1) - 1
    n_chunks = x_ref.shape[0] // rows

    def chunk(c):
        return pl.ds(c * rows, rows)

    def norm(c):
        h_ref[chunk(c), :] = (_rms_scale(x_ref[chunk(c), :], RMS_EPS) * gn_ref[...]).astype(BF16)

    def run(first, final):
        acts = {}
        for i in range(n_chunks + 2):
            if first and i < n_chunks:
                norm(i)
            if 1 <= i <= n_chunks:
                h = h_ref[chunk(i - 1), :]
                gate = jnp.dot(h, wg_ref[...], preferred_element_type=F32)
                up = jnp.dot(h, wu_ref[...], preferred_element_type=F32)
                acts[i - 1] = (gate / (1.0 + jnp.exp(-gate)) * up).astype(BF16)
            if i >= 2:
                dst = chunk(i - 2)
                base = x_ref[dst, :] if first else o_ref[dst, :]
                acc = base + jnp.dot(acts.pop(i - 2), wd_ref[...], preferred_element_type=F32)
                o_ref[dst, :] = _rms_scale(acc, RMS_EPS) * gf_ref[...] if final else acc

    pl.when(j == 0)(lambda: run(True, False))
    pl.when((j > 0) & (j < last))(lambda: run(False, False))
    pl.when(j == last)(lambda: run(False, True))


def _ffn(x2d, gain, wg_bf16, wu_bf16, wd_bf16, gain_final, tm=1024, tf=512, rows=256):
    t = x2d.shape[0]
    assert D_FF // tf >= 2
    return pl.pallas_call(
        functools.partial(_ffn_kernel, rows=rows),
        grid=(t // tm, D_FF // tf),
        in_specs=[
            pl.BlockSpec((tm, D_MODEL), lambda m, j: (m, 0)),
            pl.BlockSpec((1, D_MODEL), lambda m, j: (0, 0)),
            pl.BlockSpec((D_MODEL, tf), lambda m, j: (0, j)),
            pl.BlockSpec((D_MODEL, tf), lambda m, j: (0, j)),
            pl.BlockSpec((tf, D_MODEL), lambda m, j: (j, 0)),
            pl.BlockSpec((1, D_MODEL), lambda m, j: (0, 0)),
        ],
        out_specs=pl.BlockSpec((tm, D_MODEL), lambda m, j: (m, 0)),
        out_shape=jax.ShapeDtypeStruct((t, D_MODEL), F32),
        scratch_shapes=[pltpu.VMEM((tm, D_MODEL), BF16)],
        compiler_params=_params(("arbitrary", "arbitrary")),
        name="ffn",
    )(x2d, gain, wg_bf16, wu_bf16, wd_bf16, gain_final)


def kernel(x, norm_attn, w_in, lambda_qk, subln, w_out, norm_ffn, w_gate, w_up, w_down, norm_final):
    batch, seq, d_model = x.shape
    assert d_model == D_MODEL and w_in.shape == (1, D_MODEL, 6 * GROUP_WIDTH)
    assert w_gate.shape == (1, D_MODEL, D_FF) and seq % 256 == 0
    x2d = x.reshape(batch * seq, D_MODEL)
    tabs = _rope_tables(seq)

    proj_a, proj_b = _in_proj(x2d, norm_attn[0][None, :], w_in[0].astype(BF16), tabs, seq)
    ya = _attn_a(proj_a, batch, seq)
    yb, (wo_b, wg_b, wu_b, wd_b) = _attn_b(proj_b, lambda_qk[0], subln[0][:, None],
                                           (w_out[0], w_gate[0], w_up[0], w_down[0]), batch, seq)
    x1 = _out_proj(ya, yb, wo_b, x2d)
    out = _ffn(x1, norm_ffn[0][None, :], wg_b, wu_b, wd_b, norm_final[None, :])
    return out.reshape(batch, seq, D_MODEL)
```
